```python
import math
import jax, jax.numpy as jnp
from jax import lax
import numpy as np

D_MODEL = 2048
BATCH = 4
SEQ = 4096
DEPTH = 2

HG_HEADS = 8
HG_DK = 128
HG_DV = 128
HG_WIDTH = HG_HEADS * HG_DK
HG_CHUNK = 64
F_MIN = 1e-30
MB_HEADS = 8
MB_HD = 128
MB_WIDTH = MB_HEADS * MB_HD
MB_BLOCK = 256
MB_TOPK = 3
MB_QCHUNK = 16
D_FF = 5632
MACARON_W = 0.5
N_SUB = 3
N_IN = 4 * HG_WIDTH + 3 * MB_WIDTH + 2 * D_MODEL
DN_ALPHA = (2 * DEPTH) ** 0.25
DN_BETA = (8 * DEPTH) ** -0.25
LN_EPS = 1e-5
RMS_EPS = 1e-6
NEG = -1e30

kernel_name = "hybrid_hgrn2_moba_macaron_deepnorm_adaln"


def layer_norm(x, g, b):
    xf = x.astype(jnp.float32)
    mu = xf.mean(-1, keepdims=True)
    var = jnp.square(xf - mu).mean(-1, keepdims=True)
    return ((xf - mu) * lax.rsqrt(var + LN_EPS) * g.astype(jnp.float32) + b.astype(jnp.float32)).astype(x.dtype)


def modulate(x, shift, scale):
    return x * (1 + scale[:, None, :]) + shift[:, None, :]


def swiglu(h, w_gate, w_up, w_down):
    return (jax.nn.silu(h @ w_gate) * (h @ w_up)) @ w_down


def hgrn2_mixer(q, f_logit, i, g, lb, norm_g):
    B, S, _ = q.shape
    nch = S // HG_CHUNK
    f32 = jnp.float32
    lb = lb.astype(f32)
    z = f_logit.astype(f32)
    f = lb + (1 - lb) * jax.nn.sigmoid(z)
    log_f = jnp.log(jnp.maximum(f, F_MIN))
    k = (1 - lb) * jax.nn.sigmoid(-z)
    qf = jax.nn.silu(q.astype(f32))

    def chunks(t):
        return t.reshape(B, nch, HG_CHUNK, HG_HEADS, -1).transpose(1, 0, 3, 2, 4)

    tri = jnp.tril(jnp.ones((HG_CHUNK, HG_CHUNK), bool))

    def step(state, inp):
        qc, kc, vc, lfc = inp
        b = jnp.cumsum(lfc, axis=2)
        diff = b[:, :, :, None, :] - b[:, :, None, :, :]
        decay = jnp.exp(jnp.where(tri[:, :, None], diff, NEG))
        attn = jnp.einsum('bhtd,bhsd,bhtsd->bhts', qc, kc, decay)
        o = attn @ vc + jnp.einsum('bhtd,bhde->bhte', qc * jnp.exp(b), state)
        b_last = b[:, :, -1:, :]
        new_state = jnp.exp(b_last[:, :, 0, :])[..., None] * state + jnp.einsum(
            'bhsd,bhse->bhde', kc * jnp.exp(b_last - b), vc)
        return new_state, o

    state0 = jnp.zeros((B, HG_HEADS, HG_DK, HG_DV), f32)
    _, o = lax.scan(step, state0, (chunks(qf), chunks(k), chunks(i.astype(f32)), chunks(log_f)))
    o = o.transpose(1, 0, 3, 2, 4).reshape(B, S, HG_HEADS, HG_DV)
    o = o * lax.rsqrt(jnp.mean(jnp.square(o), -1, keepdims=True) + RMS_EPS)
    o = o * norm_g.astype(f32).reshape(HG_HEADS, HG_DV)
    o = o * jax.nn.silu(g.astype(f32)).reshape(B, S, HG_HEADS, HG_DV)
    return o.reshape(B, S, HG_WIDTH).astype(q.dtype)


def moba_mixer(q, k, v):
    B, S, _ = q.shape
    f32 = jnp.float32
    nb = -(-S // MB_BLOCK)
    s_pad = nb * MB_BLOCK
    k_sel = min(MB_TOPK, nb)

    def heads(t):
        t = t.reshape(B, S, MB_HEADS, MB_HD).transpose(0, 2, 1, 3)
        return jnp.pad(t, ((0, 0), (0, 0), (0, s_pad - S), (0, 0)))

    qh, kh, vh = heads(q), heads(k), heads(v)
    kb = kh.reshape(B, MB_HEADS, nb, MB_BLOCK, MB_HD)
    vb = vh.reshape(B, MB_HEADS, nb, MB_BLOCK, MB_HD)
    k_mean = kb.astype(f32).mean(3)
    gate = jnp.einsum('bhsd,bhnd->bhsn', qh.astype(f32), k_mean)
    own_blk = jnp.arange(s_pad) // MB_BLOCK
    past = jnp.arange(nb)[None, :] < own_blk[:, None]
    gate = jnp.where(past, gate, NEG)
    _, idx = lax.top_k(gate, k_sel)
    valid = idx < own_blk[None, None, :, None]

    nq = s_pad // MB_QCHUNK

    def qchunks(t):
        return t.reshape(B, MB_HEADS, nq, MB_QCHUNK, t.shape[-1]).transpose(2, 0, 1, 3, 4)

    starts = jnp.arange(nq) * MB_QCHUNK
    b_ix = jnp.arange(B)[:, None, None, None]
    h_ix = jnp.arange(MB_HEADS)[None, :, None, None]
    scale = MB_HD ** -0.5

    def attend(args):
        qc, idc, vdc, start = args
        kg = kb[b_ix, h_ix, idc]
        vg = vb[b_ix, h_ix, idc]
        blk = start // MB_BLOCK
        k_own = lax.dynamic_index_in_dim(kb, blk, axis=2, keepdims=False)
        v_own = lax.dynamic_index_in_dim(vb, blk, axis=2, keepdims=False)
        s_sel = jnp.einsum('bhqd,bhqnpd->bhqnp', qc, kg).astype(f32) * scale
        s_sel = jnp.where(vdc[..., None], s_sel, NEG).reshape(B, MB_HEADS, MB_QCHUNK, k_sel * MB_BLOCK)
        s_own = jnp.einsum('bhqd,bhpd->bhqp', qc, k_own).astype(f32) * scale
        q_pos = start + jnp.arange(MB_QCHUNK)
        k_pos = blk * MB_BLOCK + jnp.arange(MB_BLOCK)
        s_own = jnp.where(k_pos[None, :] <= q_pos[:, None], s_own, NEG)
        p = jax.nn.softmax(jnp.concatenate([s_sel, s_own], axis=-1), axis=-1)
        p_sel = p[..., :k_sel * MB_BLOCK].reshape(B, MB_HEADS, MB_QCHUNK, k_sel, MB_BLOCK).astype(vg.dtype)
        p_own = p[..., k_sel * MB_BLOCK:].astype(v_own.dtype)
        return (jnp.einsum('bhqnp,bhqnpd->bhqd', p_sel, vg)
                + jnp.einsum('bhqp,bhpd->bhqd', p_own, v_own))

    out = lax.map(attend, (qchunks(qh), qchunks(idx), qchunks(valid), starts))
    out = out.transpose(1, 0, 3, 2, 4).reshape(B, s_pad, MB_WIDTH)
    return out[:, :S]


def hybrid_mixer(h, w_in, lb, norm_g, w_branch_a, w_branch_b, w_out):
    sizes = [HG_WIDTH] * 4 + [MB_WIDTH] * 3 + [D_MODEL] * 2
    cuts = [int(s) for s in np.cumsum(sizes)[:-1]]
    proj = h @ w_in
    hq, hf, hi, hg, mq, mk, mv, ga, gb = jnp.split(proj, cuts, axis=-1)
    y_a = hgrn2_mixer(hq, hf, hi, hg, lb, norm_g) @ w_branch_a
    y_b = moba_mixer(mq, mk, mv) @ w_branch_b
    merged = jax.nn.sigmoid(ga) * y_a + jax.nn.sigmoid(gb) * y_b
    return merged @ w_out


def setup_inputs(seed: int = 0) -> dict:
    key = jax.random.key(seed)
    ks = jax.random.split(key, 16)
    nrm = jax.random.normal
    f32 = jnp.float32
    x = nrm(ks[0], (BATCH, SEQ, D_MODEL), f32)
    c = nrm(ks[1], (BATCH, D_MODEL), f32)
    ada_w = nrm(ks[2], (DEPTH, D_MODEL, N_SUB * 3 * D_MODEL), f32) * (0.1 * D_MODEL ** -0.5)
    ada_b = nrm(ks[3], (DEPTH, N_SUB * 3 * D_MODEL), f32) * 0.01
    ln_g = 1.0 + 0.02 * nrm(ks[4], (DEPTH, N_SUB, D_MODEL), f32)
    ln_b = 0.02 * nrm(ks[5], (DEPTH, N_SUB, D_MODEL), f32)
    ffn_w_gate = nrm(ks[6], (DEPTH, 2, D_MODEL, D_FF), f32) * D_MODEL ** -0.5
    ffn_w_up = nrm(ks[7], (DEPTH, 2, D_MODEL, D_FF), f32) * D_MODEL ** -0.5
    ffn_w_down = nrm(ks[8], (DEPTH, 2, D_FF, D_MODEL), f32) * (D_FF ** -0.5 * DN_BETA)
    w_in = nrm(ks[9], (DEPTH, D_MODEL, N_IN), f32) * D_MODEL ** -0.5
    hg_lb_logits = 0.5 * nrm(ks[10], (DEPTH, HG_WIDTH), f32)
    hg_norm_g = 1.0 + 0.02 * nrm(ks[11], (DEPTH, HG_WIDTH), f32)
    w_branch_a = nrm(ks[12], (DEPTH, HG_WIDTH, D_MODEL), f32) * HG_WIDTH ** -0.5
    w_branch_b = nrm(ks[13], (DEPTH, MB_WIDTH, D_MODEL), f32) * MB_WIDTH ** -0.5
    w_out = nrm(ks[14], (DEPTH, D_MODEL, D_MODEL), f32) * (D_MODEL ** -0.5 * DN_BETA)
    return {"x": x, "c": c, "ada_w": ada_w, "ada_b": ada_b, "ln_g": ln_g, "ln_b": ln_b,
            "ffn_w_gate": ffn_w_gate, "ffn_w_up": ffn_w_up, "ffn_w_down": ffn_w_down,
            "w_in": w_in, "hg_lb_logits": hg_lb_logits, "hg_norm_g": hg_norm_g,
            "w_branch_a": w_branch_a, "w_branch_b": w_branch_b, "w_out": w_out}


def reference(x, c, ada_w, ada_b, ln_g, ln_b, ffn_w_gate, ffn_w_up, ffn_w_down,
              w_in, hg_lb_logits, hg_norm_g, w_branch_a, w_branch_b, w_out):
    B = x.shape[0]
    p_lb = jax.nn.softmax(hg_lb_logits.astype(jnp.float32), axis=0)
    lower_bounds = jnp.cumsum(p_lb, axis=0) - p_lb[0]
    cond = jax.nn.silu(c)
    for l in range(DEPTH):
        mod = (cond @ ada_w[l] + ada_b[l]).reshape(B, N_SUB, 3, D_MODEL)
        shift, scale, gate = mod[:, :, 0], mod[:, :, 1], mod[:, :, 2]
        h = modulate(x, shift[:, 0], scale[:, 0])
        y = swiglu(h, ffn_w_gate[l, 0], ffn_w_up[l, 0], ffn_w_down[l, 0])
        x = layer_norm(DN_ALPHA * x + MACARON_W * (1 + gate[:, 0, None, :]) * y, ln_g[l, 0], ln_b[l, 0])
        h = modulate(x, shift[:, 1], scale[:, 1])
        y = hybrid_mixer(h, w_in[l], lower_bounds[l], hg_norm_g[l], w_branch_a[l], w_branch_b[l], w_out[l])
        x = layer_norm(DN_ALPHA * x + (1 + gate[:, 1, None, :]) * y, ln_g[l, 1], ln_b[l, 1])
        h = modulate(x, shift[:, 2], scale[:, 2])
        y = swiglu(h, ffn_w_gate[l, 1], ffn_w_up[l, 1], ffn_w_down[l, 1])
        x = layer_norm(DN_ALPHA * x + MACARON_W * (1 + gate[:, 2, None, :]) * y, ln_g[l, 2], ln_b[l, 2])
    return x
```

```python
import functools
import math

import jax
import jax.numpy as jnp
from jax import lax
from jax.experimental import pallas as pl
from jax.experimental.pallas import tpu as pltpu

F32 = jnp.float32
BF16 = jnp.bfloat16

HEAD = 128
MB_BLOCK = 256
MB_TOPK = 3
HG_CHUNK = 64
F_MIN = 1e-30
MACARON_W = 0.5
N_SUB = 3
LN_EPS = 1e-5
RMS_EPS = 1e-6
NEG = -1e30
LANES = 128
VMEM_LIMIT = 56 * 1024 * 1024


def _cparams(sem):
    return pltpu.CompilerParams(dimension_semantics=sem, vmem_limit_bytes=VMEM_LIMIT)


def _dot(a, b):
    return jnp.dot(a, b, preferred_element_type=F32)


def _dot_nt(a, b):
    return lax.dot_general(a, b, (((1,), (1,)), ((), ())), preferred_element_type=F32)


def _dot_tn(a, b):
    return lax.dot_general(a, b, (((0,), (0,)), ((), ())), preferred_element_type=F32)


def _silu(x):
    return x * jax.nn.sigmoid(x)


def _layer_norm(z, g, b):
    mu = jnp.mean(z, axis=-1, keepdims=True)
    zc = z - mu
    var = jnp.mean(zc * zc, axis=-1, keepdims=True)
    return zc * lax.rsqrt(var + LN_EPS) * g + b


def _mod_rows(mod_ref, sub):
    return (mod_ref[0, 3 * sub:3 * sub + 1, :], mod_ref[0, 3 * sub + 1:3 * sub + 2, :],
            mod_ref[0, 3 * sub + 2:3 * sub + 3, :])


def _adaln_kernel(c_ref, w_ref, b_ref, o_ref):
    cond = _silu(c_ref[...])
    o_ref[...] = _dot(cond, w_ref[...]) + b_ref[...]


def _adaln(c_pad, ada_w, ada_b):
    depth, d, n = ada_w.shape
    tn = min(1024, d)
    return pl.pallas_call(
        _adaln_kernel,
        out_shape=jax.ShapeDtypeStruct((depth, c_pad.shape[0], n), F32),
        grid=(depth, n // tn),
        in_specs=[pl.BlockSpec(c_pad.shape, lambda l, j: (0, 0)),
                  pl.BlockSpec((None, d, tn), lambda l, j: (l, 0, j)),
                  pl.BlockSpec((None, 1, tn), lambda l, j: (l, 0, j))],
        out_specs=pl.BlockSpec((None, c_pad.shape[0], tn), lambda l, j: (l, 0, j)),
        compiler_params=_cparams(("arbitrary", "arbitrary")),
        name="adaln",
    )(c_pad, ada_w, ada_b.reshape(depth, 1, n))


def _ffn_kernel(x_ref, mod_ref, wg_ref, wu_ref, wd_ref, lng_ref, lnb_ref, o_ref, h_scr, acc_scr,
                *, sub, alpha, nf):
    f = pl.program_id(1)
    shift, scale, gate = _mod_rows(mod_ref, sub)

    @pl.when(f == 0)
    def _():
        h_scr[...] = (x_ref[...] * (1.0 + scale) + shift).astype(BF16)
        acc_scr[...] = jnp.zeros_like(acc_scr)

    h = h_scr[...]
    g = _dot(h, wg_ref[...])
    u = _dot(h, wu_ref[...])
    a = (_silu(g) * u).astype(BF16)
    acc_scr[...] += _dot(a, wd_ref[...])

    @pl.when(f == nf - 1)
    def _():
        z = alpha * x_ref[...] + MACARON_W * (1.0 + gate) * acc_scr[...]
        o_ref[...] = _layer_norm(z, lng_ref[...], lnb_ref[...])


def _ffn(x2, mod_l, wg, wu, wd, lng, lnb, *, layer, which, sub, seq, alpha):
    t, d = x2.shape
    dff = wg.shape[-1]
    tm = min(512, seq)
    tf = 512 if dff % 512 == 0 else dff
    nf = dff // tf
    spt = seq // tm
    kern = functools.partial(_ffn_kernel, sub=sub, alpha=alpha, nf=nf)
    return pl.pallas_call(
        kern,
        out_shape=jax.ShapeDtypeStruct((t, d), F32),
        grid=(t // tm, nf),
        in_specs=[pl.BlockSpec((tm, d), lambda m, f: (m, 0)),
                  pl.BlockSpec((1, 3 * N_SUB, d), lambda m, f: (m // spt, 0, 0)),
                  pl.BlockSpec((None, None, d, tf), lambda m, f: (layer, which, 0, f)),
                  pl.BlockSpec((None, None, d, tf), lambda m, f: (layer, which, 0, f)),
                  pl.BlockSpec((None, None, tf, d), lambda m, f: (layer, which, f, 0)),
                  pl.BlockSpec((1, d), lambda m, f: (0, 0)),
                  pl.BlockSpec((1, d), lambda m, f: (0, 0))],
        out_specs=pl.BlockSpec((tm, d), lambda m, f: (m, 0)),
        scratch_shapes=[pltpu.VMEM((tm, d), BF16), pltpu.VMEM((tm, d), F32)],
        compiler_params=_cparams(("parallel", "arbitrary")),
        name=f"ffn_l{layer}_{which}",
    )(x2, mod_l, wg, wu, wd, lng, lnb)


def _proj_hg_kernel(x_ref, mod_ref, lbl_ref, wq_ref, wf_ref, wi_ref, wg_ref,
                    q_ref, lf_ref, k_ref, v_ref, sg_ref, *, layer):
    shift, scale, _ = _mod_rows(mod_ref, 1)
    h = (x_ref[...] * (1.0 + scale) + shift).astype(BF16)
    logits = lbl_ref[...]
    e = jnp.exp(logits - jnp.max(logits, axis=0, keepdims=True))
    p = e / jnp.sum(e, axis=0, keepdims=True)
    lb = jnp.zeros_like(p[0:1, :])
    for i in range(1, layer + 1):
        lb = lb + p[i:i + 1, :]
    q_ref[...] = _silu(_dot(h, wq_ref[...])).astype(BF16)
    z = _dot(h, wf_ref[...])
    fgate = lb + (1.0 - lb) * jax.nn.sigmoid(z)
    lf_ref[...] = jnp.log(jnp.maximum(fgate, F_MIN))
    k_ref[...] = ((1.0 - lb) * jax.nn.sigmoid(-z)).astype(BF16)
    v_ref[...] = _dot(h, wi_ref[...]).astype(BF16)
    sg_ref[...] = _silu(_dot(h, wg_ref[...])).astype(BF16)


def _proj_hg(x2, mod_l, lb_logits, w_in, *, layer, seq, hgw):
    t, d = x2.shape
    tm = min(512, seq)
    spt = seq // tm
    depth = lb_logits.shape[0]

    def wspec(col):
        return pl.BlockSpec((None, d, hgw), lambda m, col=col: (layer, 0, col), pipeline_mode=pl.Buffered(1))

    out = jax.ShapeDtypeStruct((t, hgw), BF16)
    ospec = pl.BlockSpec((tm, hgw), lambda m: (m, 0))
    return pl.pallas_call(
        functools.partial(_proj_hg_kernel, layer=layer),
        out_shape=(out, jax.ShapeDtypeStruct((t, hgw), F32), out, out, out),
        grid=(t // tm,),
        in_specs=[pl.BlockSpec((tm, d), lambda m: (m, 0)),
                  pl.BlockSpec((1, 3 * N_SUB, d), lambda m: (m // spt, 0, 0)),
                  pl.BlockSpec((depth, hgw), lambda m: (0, 0)),
                  wspec(0), wspec(1), wspec(2), wspec(3)],
        out_specs=(ospec, ospec, ospec, ospec, ospec),
        compiler_params=_cparams(("parallel",)),
        name=f"proj_hg_l{layer}",
    )(x2, mod_l, lb_logits, w_in, w_in, w_in, w_in)


def _proj_mb_kernel(x_ref, mod_ref, wq_ref, wk_ref, wv_ref, q_ref, k_ref, v_ref):
    shift, scale, _ = _mod_rows(mod_ref, 1)
    h = (x_ref[...] * (1.0 + scale) + shift).astype(BF16)
    q_ref[...] = _dot(h, wq_ref[...]).astype(BF16)
    k_ref[...] = _dot(h, wk_ref[...]).astype(BF16)
    v_ref[...] = _dot(h, wv_ref[...]).astype(BF16)


def _proj_mb(x2, mod_l, w_in, *, layer, seq, mbw, col0):
    t, d = x2.shape
    tm = min(512, seq)
    spt = seq // tm

    def wspec(col):
        return pl.BlockSpec((None, d, mbw), lambda m, col=col: (layer, 0, col), pipeline_mode=pl.Buffered(1))

    out = jax.ShapeDtypeStruct((t, mbw), BF16)
    ospec = pl.BlockSpec((tm, mbw), lambda m: (m, 0))
    return pl.pallas_call(
        _proj_mb_kernel,
        out_shape=(out, out, out),
        grid=(t // tm,),
        in_specs=[pl.BlockSpec((tm, d), lambda m: (m, 0)),
                  pl.BlockSpec((1, 3 * N_SUB, d), lambda m: (m // spt, 0, 0)),
                  wspec(col0), wspec(col0 + 1), wspec(col0 + 2)],
        out_specs=(ospec, ospec, ospec),
        compiler_params=_cparams(("parallel",)),
        name=f"proj_mb_l{layer}",
    )(x2, mod_l, w_in, w_in, w_in)


def _bcast_left_end(w, c, row):
    n = w.shape[0]
    if c >= 8:
        parts = []
        for g in range(n // (2 * c)):
            r = g * 2 * c + c - 1
            parts.append(jnp.broadcast_to(w[r:r + 1, :], (2 * c, w.shape[1])))
        return parts[0] if len(parts) == 1 else jnp.concatenate(parts, axis=0)
    up = lambda x, s: pltpu.roll(x, n - s, 0)
    down = lambda x, s: pltpu.roll(x, s, 0)
    if c == 1:
        return jnp.where((row & 1) == 1, down(w, 1), w)
    if c == 2:
        z = jnp.where((row & 3) == 0, up(w, 1), w)
        return jnp.where((row & 3) >= 2, down(z, 2), z)
    z = jnp.where((row & 7) == 2, up(w, 1), w)
    z = jnp.where((row & 7) < 2, up(z, 2), z)
    return jnp.where((row & 7) >= 4, down(z, 4), z)


def _hgrn_kernel(q_ref, lf_ref, k_ref, v_ref, sg_ref, ng_ref, o_ref, st_scr, *, chunk, nchunk):
    @pl.when(pl.program_id(2) == 0)
    def _():
        st_scr[...] = jnp.zeros_like(st_scr)

    row = lax.broadcasted_iota(jnp.int32, (chunk, HEAD), 0)
    rr = lax.broadcasted_iota(jnp.int32, (chunk, chunk), 0)
    cc = lax.broadcasted_iota(jnp.int32, (chunk, chunk), 1)
    nlev = int(math.log2(chunk))
    ng = ng_ref[...]

    def body(ci, carry):
        rows = pl.ds(pl.multiple_of(ci * chunk, chunk), chunk)
        w = lf_ref[rows, :]
        q = q_ref[rows, :].astype(F32)
        k = k_ref[rows, :].astype(F32)
        v = v_ref[rows, :]
        attn = jnp.where(rr == cc, _dot_nt(q.astype(BF16), k.astype(BF16)), 0.0)
        for lev in range(nlev):
            c = 1 << lev
            right = (row & c) != 0
            wl = _bcast_left_end(w, c, row)
            e = jnp.exp(jnp.where(right, w, wl - w))
            qk = jnp.where(right, q, k) * e
            ql = jnp.where(right, qk, 0.0).astype(BF16)
            kl = jnp.where(right, 0.0, qk).astype(BF16)
            p = _dot_nt(ql, kl)
            if 2 * c < chunk:
                p = jnp.where((rr >> (lev + 1)) == (cc >> (lev + 1)), p, 0.0)
            attn = attn + p
            w = jnp.where(right, w + wl, w)
        bl = w[chunk - 1:chunk, :]
        qe = (q * jnp.exp(w)).astype(BF16)
        ke = (k * jnp.exp(bl - w)).astype(BF16)
        st = st_scr[...]
        o = _dot(attn.astype(BF16), v) + _dot_nt(qe, st.astype(BF16))
        st_scr[...] = st * jnp.exp(bl) + _dot_tn(v, ke)
        o = o * lax.rsqrt(jnp.mean(o * o, axis=-1, keepdims=True) + RMS_EPS)
        o = o * ng * sg_ref[rows, :].astype(F32)
        o_ref[rows, :] = o.astype(BF16)
        return carry

    lax.fori_loop(0, nchunk, body, 0)


def _hgrn(qf, lf, kk, vv, sg, norm_g, *, layer, batch, seq):
    t, hgw = qf.shape
    heads = hgw // HEAD
    lb = min(512, seq)
    chunk = min(HG_CHUNK, lb)
    nsb = seq // lb
    spec = pl.BlockSpec((lb, HEAD), lambda b, h, s: (b * nsb + s, h))
    return pl.pallas_call(
        functools.partial(_hgrn_kernel, chunk=chunk, nchunk=lb // chunk),
        out_shape=jax.ShapeDtypeStruct((t, hgw), BF16),
        grid=(batch, heads, nsb),
        in_specs=[spec, spec, spec, spec, spec,
                  pl.BlockSpec((None, 1, HEAD), lambda b, h, s: (layer, 0, h))],
        out_specs=spec,
        scratch_shapes=[pltpu.VMEM((HEAD, HEAD), F32)],
        compiler_params=_cparams(("parallel", "parallel", "arbitrary")),
        name=f"hgrn_l{layer}",
    )(qf, lf, kk, vv, sg, norm_g.reshape(norm_g.shape[0], 1, hgw))


def _moba_kernel(q_ref, k_ref, v_ref, o_ref, kmh_scr, kml_scr, *, nblk, blk):
    i = pl.program_id(2)

    @pl.when(i == 0)
    def _():
        kmh_scr[...] = jnp.zeros_like(kmh_scr)
        kml_scr[...] = jnp.zeros_like(kml_scr)
        for j in range(nblk):
            km = jnp.sum(k_ref[j * blk:(j + 1) * blk, :].astype(F32), axis=0, keepdims=True) * (1.0 / blk)
            hi = km.astype(BF16)
            kmh_scr[j:j + 1, :] = hi
            kml_scr[j:j + 1, :] = (km - hi.astype(F32)).astype(BF16)

    q = q_ref[...]
    scale = HEAD ** -0.5
    gate = _dot_nt(q, kmh_scr[...]) + _dot_nt(q, kml_scr[...])
    lane = lax.broadcasted_iota(jnp.int32, gate.shape, 1)
    lane_f = lane.astype(F32)
    i_f = i.astype(F32)
    g = jnp.where(lane < i, gate, NEG)
    picks = []
    for _ in range(MB_TOPK):
        m = jnp.max(g, axis=1, keepdims=True)
        idx = jnp.min(jnp.where(g == m, lane_f, float(LANES)), axis=1, keepdims=True)
        picks.append(jnp.where(idx < i_f, idx, -1.0))
        g = jnp.where(lane_f == idx, -3e38, g)

    rr = lax.broadcasted_iota(jnp.int32, (blk, blk), 0)
    cc = lax.broadcasted_iota(jnp.int32, (blk, blk), 1)
    own = pl.ds(pl.multiple_of(i * blk, blk), blk)
    s = _dot_nt(q, k_ref[own, :]) * scale
    s = jnp.where(cc <= rr, s, NEG)
    m0 = jnp.max(s, axis=1, keepdims=True)
    p = jnp.exp(s - m0)
    l0 = jnp.sum(p, axis=1, keepdims=True)
    acc0 = _dot(p.astype(BF16), v_ref[own, :])

    def body(j, carry):
        m_prev, l_prev, acc = carry
        j_f = j.astype(F32)
        sel = (picks[0] == j_f) | (picks[1] == j_f) | (picks[2] == j_f)
        rows = pl.ds(pl.multiple_of(j * blk, blk), blk)
        sj = _dot_nt(q, k_ref[rows, :]) * scale
        sj = jnp.where(sel, sj, NEG)
        m_new = jnp.maximum(m_prev, jnp.max(sj, axis=1, keepdims=True))
        a = jnp.exp(m_prev - m_new)
        pj = jnp.exp(sj - m_new)
        l_new = a * l_prev + jnp.sum(pj, axis=1, keepdims=True)
        acc = a * acc + _dot(pj.astype(BF16), v_ref[rows, :])
        return m_new, l_new, acc

    _, l_fin, acc = lax.fori_loop(0, i, body, (m0, l0, acc0))
    o_ref[...] = (acc / l_fin).astype(BF16)


def _moba(mq, mk, mv, *, layer, batch, seq):
    t, mbw = mq.shape
    heads = mbw // HEAD
    blk = MB_BLOCK
    nblk = seq // blk
    assert seq % blk == 0 and nblk <= LANES
    qspec = pl.BlockSpec((blk, HEAD), lambda b, h, i: (b * nblk + i, h))
    kvspec = pl.BlockSpec((seq, HEAD), lambda b, h, i: (b, h))
    return pl.pallas_call(
        functools.partial(_moba_kernel, nblk=nblk, blk=blk),
        out_shape=jax.ShapeDtypeStruct((t, mbw), BF16),
        grid=(batch, heads, nblk),
        in_specs=[qspec, kvspec, kvspec],
        out_specs=qspec,
        scratch_shapes=[pltpu.VMEM((LANES, HEAD), BF16), pltpu.VMEM((LANES, HEAD), BF16)],
        compiler_params=_cparams(("parallel", "parallel", "arbitrary")),
        name=f"moba_l{layer}",
    )(mq, mk, mv)


def _merge_kernel(x_ref, mod_ref, oa_ref, ob_ref, wga_ref, wgb_ref, wa_ref, wb_ref, wo_ref,
                  lng_ref, lnb_ref, o_ref, *, alpha):
    shift, scale, gate = _mod_rows(mod_ref, 1)
    x = x_ref[...]
    h = (x * (1.0 + scale) + shift).astype(BF16)
    ya = _dot(oa_ref[...], wa_ref[...])
    merged = jax.nn.sigmoid(_dot(h, wga_ref[...])) * ya
    yb = _dot(ob_ref[...], wb_ref[...])
    merged = merged + jax.nn.sigmoid(_dot(h, wgb_ref[...])) * yb
    y = _dot(merged.astype(BF16), wo_ref[...])
    z = alpha * x + (1.0 + gate) * y
    o_ref[...] = _layer_norm(z, lng_ref[...], lnb_ref[...])


def _merge(x2, mod_l, oa, ob, w_gates, wa, wb, wo, lng, lnb, *, layer, seq, alpha):
    t, d = x2.shape
    hgw, mbw = oa.shape[1], ob.shape[1]
    tm = min(256, seq)
    spt = seq // tm
    one = pl.Buffered(1)
    return pl.pallas_call(
        functools.partial(_merge_kernel, alpha=alpha),
        out_shape=jax.ShapeDtypeStruct((t, d), F32),
        grid=(t // tm,),
        in_specs=[pl.BlockSpec((tm, d), lambda m: (m, 0)),
                  pl.BlockSpec((1, 3 * N_SUB, d), lambda m: (m // spt, 0, 0)),
                  pl.BlockSpec((tm, hgw), lambda m: (m, 0)),
                  pl.BlockSpec((tm, mbw), lambda m: (m, 0)),
                  pl.BlockSpec((None, d, d), lambda m: (layer, 0, 0), pipeline_mode=one),
                  pl.BlockSpec((None, d, d), lambda m: (layer, 0, 1), pipeline_mode=one),
                  pl.BlockSpec((None, hgw, d), lambda m: (layer, 0, 0), pipeline_mode=one),
                  pl.BlockSpec((None, mbw, d), lambda m: (layer, 0, 0), pipeline_mode=one),
                  pl.BlockSpec((None, d, d), lambda m: (layer, 0, 0), pipeline_mode=one),
                  pl.BlockSpec((1, d), lambda m: (0, 0)),
                  pl.BlockSpec((1, d), lambda m: (0, 0))],
        out_specs=pl.BlockSpec((tm, d), lambda m: (m, 0)),
        compiler_params=_cparams(("parallel",)),
        name=f"merge_l{layer}",
    )(x2, mod_l, oa, ob, w_gates, w_gates, wa, wb, wo, lng, lnb)


def kernel(x, c, ada_w, ada_b, ln_g, ln_b, ffn_w_gate, ffn_w_up, ffn_w_down, w_in, hg_lb_logits,
           hg_norm_g, w_branch_a, w_branch_b, w_out):
    batch, seq, d = x.shape
    depth = ada_w.shape[0]
    hgw = hg_norm_g.shape[1]
    mbw = w_branch_b.shape[1]
    assert w_in.shape[2] == 4 * hgw + 3 * mbw + 2 * d and hgw == mbw and d % hgw == 0
    alpha = (2 * depth) ** 0.25

    wg, wu, wd = ffn_w_gate.astype(BF16), ffn_w_up.astype(BF16), ffn_w_down.astype(BF16)
    nmix = 4 * hgw + 3 * mbw
    win, wgates = w_in[:, :, :nmix].astype(BF16), w_in[:, :, nmix:].astype(BF16)
    wa, wb, wo = w_branch_a.astype(BF16), w_branch_b.astype(BF16), w_out.astype(BF16)

    c_pad = jnp.pad(c, ((0, 8 - batch % 8 if batch % 8 else 0), (0, 0)))
    mod = _adaln(c_pad, ada_w, ada_b)[:, :batch].reshape(depth, batch, 3 * N_SUB, d)

    x2 = x.reshape(batch * seq, d)
    for l in range(depth):
        mod_l = mod[l]
        lng = [ln_g[l, j].reshape(1, d) for j in range(N_SUB)]
        lnb = [ln_b[l, j].reshape(1, d) for j in range(N_SUB)]
        x2 = _ffn(x2, mod_l, wg, wu, wd, lng[0], lnb[0], layer=l, which=0, sub=0, seq=seq, alpha=alpha)
        qf, lf, kk, vv, sg = _proj_hg(x2, mod_l, hg_lb_logits, win, layer=l, seq=seq, hgw=hgw)
        mq, mk, mv = _proj_mb(x2, mod_l, win, layer=l, seq=seq, mbw=mbw, col0=4 * hgw // mbw)
        oa = _hgrn(qf, lf, kk, vv, sg, hg_norm_g, layer=l, batch=batch, seq=seq)
        ob = _moba(mq, mk, mv, layer=l, batch=batch, seq=seq)
        x2 = _merge(x2, mod_l, oa, ob, wgates, wa, wb, wo, lng[1], lnb[1], layer=l, seq=seq, alpha=alpha)
        x2 = _ffn(x2, mod_l, wg, wu, wd, lng[2], lnb[2], layer=l, which=1, sub=2, seq=seq, alpha=alpha)
    return x2.reshape(batch, seq, d)
```

```python
import functools
import math

import jax
import jax.numpy as jnp
from jax import lax
from jax.experimental import pallas as pl
from jax.experimental.pallas import tpu as pltpu

F32 = jnp.float32
BF16 = jnp.bfloat16

HEAD = 128
MB_BLOCK = 256
MB_TOPK = 3
MB_TILE_BLOCKS = 4
HG_CHUNK = 64
HG_UNROLL = 4
F_MIN = 1e-30
MACARON_W = 0.5
N_SUB = 3
LN_EPS = 1e-5
RMS_EPS = 1e-6
NEG = -1e30
LANES = 128
VMEM_LIMIT = 56 * 1024 * 1024


def _cparams(sem):
    return pltpu.CompilerParams(dimension_semantics=sem, vmem_limit_bytes=VMEM_LIMIT)


def _dot(a, b):
    return jnp.dot(a, b, preferred_element_type=F32)


def _dot_nt(a, b):
    return lax.dot_general(a, b, (((1,), (1,)), ((), ())), preferred_element_type=F32)


def _dot_tn(a, b):
    return lax.dot_general(a, b, (((0,), (0,)), ((), ())), preferred_element_type=F32)


def _silu(x):
    return x * jax.nn.sigmoid(x)


def _layer_norm(z, g, b):
    mu = jnp.mean(z, axis=-1, keepdims=True)
    zc = z - mu
    var = jnp.mean(zc * zc, axis=-1, keepdims=True)
    return zc * lax.rsqrt(var + LN_EPS) * g + b


def _mod_rows(mod_ref, sub):
    return (mod_ref[0, 3 * sub:3 * sub + 1, :], mod_ref[0, 3 * sub + 1:3 * sub + 2, :],
            mod_ref[0, 3 * sub + 2:3 * sub + 3, :])


def _adaln_kernel(c_ref, w_ref, b_ref, o_ref):
    cond = _silu(c_ref[...])
    o_ref[...] = _dot(cond, w_ref[...]) + b_ref[...]


def _adaln(c_pad, ada_w, ada_b):
    depth, d, n = ada_w.shape
    tn = min(1024, d)
    return pl.pallas_call(
        _adaln_kernel,
        out_shape=jax.ShapeDtypeStruct((depth, c_pad.shape[0], n), F32),
        grid=(depth, n // tn),
        in_specs=[pl.BlockSpec(c_pad.shape, lambda l, j: (0, 0)),
                  pl.BlockSpec((None, d, tn), lambda l, j: (l, 0, j)),
                  pl.BlockSpec((None, 1, tn), lambda l, j: (l, 0, j))],
        out_specs=pl.BlockSpec((None, c_pad.shape[0], tn), lambda l, j: (l, 0, j)),
        compiler_params=_cparams(("arbitrary", "arbitrary")),
        name="adaln",
    )(c_pad, ada_w, ada_b.reshape(depth, 1, n))


def _ffn_kernel(x_ref, mod_ref, wg_ref, wu_ref, wd_ref, lng_ref, lnb_ref, o_ref, h_scr, acc_scr,
                *, sub, alpha, nf):
    f = pl.program_id(1)
    shift, scale, gate = _mod_rows(mod_ref, sub)

    @pl.when(f == 0)
    def _():
        h_scr[...] = (x_ref[...] * (1.0 + scale) + shift).astype(BF16)
        acc_scr[...] = jnp.zeros_like(acc_scr)

    h = h_scr[...]
    g = _dot(h, wg_ref[...])
    u = _dot(h, wu_ref[...])
    a = (_silu(g) * u).astype(BF16)
    acc_scr[...] += _dot(a, wd_ref[...])

    @pl.when(f == nf - 1)
    def _():
        z = alpha * x_ref[...] + MACARON_W * (1.0 + gate) * acc_scr[...]
        o_ref[...] = _layer_norm(z, lng_ref[...], lnb_ref[...])


def _ffn(x2, mod_l, wg, wu, wd, lng, lnb, *, layer, which, sub, seq, alpha):
    t, d = x2.shape
    dff = wg.shape[-1]
    tm = min(512, seq)
    tf = 512 if dff % 512 == 0 else dff
    nf = dff // tf
    spt = seq // tm
    kern = functools.partial(_ffn_kernel, sub=sub, alpha=alpha, nf=nf)
    return pl.pallas_call(
        kern,
        out_shape=jax.ShapeDtypeStruct((t, d), F32),
        grid=(t // tm, nf),
        in_specs=[pl.BlockSpec((tm, d), lambda m, f: (m, 0)),
                  pl.BlockSpec((1, 3 * N_SUB, d), lambda m, f: (m // spt, 0, 0)),
                  pl.BlockSpec((None, None, d, tf), lambda m, f: (layer, which, 0, f)),
                  pl.BlockSpec((None, None, d, tf), lambda m, f: (layer, which, 0, f)),
                  pl.BlockSpec((None, None, tf, d), lambda m, f: (layer, which, f, 0)),
                  pl.BlockSpec((1, d), lambda m, f: (0, 0)),
                  pl.BlockSpec((1, d), lambda m, f: (0, 0))],
        out_specs=pl.BlockSpec((tm, d), lambda m, f: (m, 0)),
        scratch_shapes=[pltpu.VMEM((tm, d), BF16), pltpu.VMEM((tm, d), F32)],
        compiler_params=_cparams(("parallel", "arbitrary")),
        name=f"ffn_l{layer}_{which}",
    )(x2, mod_l, wg, wu, wd, lng, lnb)


def _proj_hg_kernel(x_ref, mod_ref, lbl_ref, wq_ref, wf_ref, wi_ref, wg_ref,
                    q_ref, lf_ref, k_ref, v_ref, sg_ref, *, layer):
    shift, scale, _ = _mod_rows(mod_ref, 1)
    h = (x_ref[...] * (1.0 + scale) + shift).astype(BF16)
    logits = lbl_ref[...]
    e = jnp.exp(logits - jnp.max(logits, axis=0, keepdims=True))
    p = e / jnp.sum(e, axis=0, keepdims=True)
    lb = jnp.zeros_like(p[0:1, :])
    for i in range(1, layer + 1):
        lb = lb + p[i:i + 1, :]
    q_ref[...] = _silu(_dot(h, wq_ref[...])).astype(BF16)
    z = _dot(h, wf_ref[...])
    fgate = lb + (1.0 - lb) * jax.nn.sigmoid(z)
    lf_ref[...] = jnp.log2(jnp.maximum(fgate, F_MIN))
    k_ref[...] = ((1.0 - lb) * jax.nn.sigmoid(-z)).astype(BF16)
    v_ref[...] = _dot(h, wi_ref[...]).astype(BF16)
    sg_ref[...] = _silu(_dot(h, wg_ref[...])).astype(BF16)


def _proj_hg(x2, mod_l, lb_logits, w_in, *, layer, seq, hgw):
    t, d = x2.shape
    tm = min(512, seq)
    spt = seq // tm
    depth = lb_logits.shape[0]

    def wspec(col):
        return pl.BlockSpec((None, d, hgw), lambda m, col=col: (layer, 0, col), pipeline_mode=pl.Buffered(1))

    out = jax.ShapeDtypeStruct((t, hgw), BF16)
    ospec = pl.BlockSpec((tm, hgw), lambda m: (m, 0))
    return pl.pallas_call(
        functools.partial(_proj_hg_kernel, layer=layer),
        out_shape=(out, jax.ShapeDtypeStruct((t, hgw), F32), out, out, out),
        grid=(t // tm,),
        in_specs=[pl.BlockSpec((tm, d), lambda m: (m, 0)),
                  pl.BlockSpec((1, 3 * N_SUB, d), lambda m: (m // spt, 0, 0)),
                  pl.BlockSpec((depth, hgw), lambda m: (0, 0)),
                  wspec(0), wspec(1), wspec(2), wspec(3)],
        out_specs=(ospec, ospec, ospec, ospec, ospec),
        compiler_params=_cparams(("parallel",)),
        name=f"proj_hg_l{layer}",
    )(x2, mod_l, lb_logits, w_in, w_in, w_in, w_in)


def _proj_mb_kernel(x_ref, mod_ref, wq_ref, wk_ref, wv_ref, q_ref, k_ref, v_ref):
    shift, scale, _ = _mod_rows(mod_ref, 1)
    h = (x_ref[...] * (1.0 + scale) + shift).astype(BF16)
    q_ref[...] = _dot(h, wq_ref[...]).astype(BF16)
    k_ref[...] = _dot(h, wk_ref[...]).astype(BF16)
    v_ref[...] = _dot(h, wv_ref[...]).astype(BF16)


def _proj_mb(x2, mod_l, w_in, *, layer, seq, mbw, col0):
    t, d = x2.shape
    tm = min(512, seq)
    spt = seq // tm

    def wspec(col):
        return pl.BlockSpec((None, d, mbw), lambda m, col=col: (layer, 0, col), pipeline_mode=pl.Buffered(1))

    out = jax.ShapeDtypeStruct((t, mbw), BF16)
    ospec = pl.BlockSpec((tm, mbw), lambda m: (m, 0))
    return pl.pallas_call(
        _proj_mb_kernel,
        out_shape=(out, out, out),
        grid=(t // tm,),
        in_specs=[pl.BlockSpec((tm, d), lambda m: (m, 0)),
                  pl.BlockSpec((1, 3 * N_SUB, d), lambda m: (m // spt, 0, 0)),
                  wspec(col0), wspec(col0 + 1), wspec(col0 + 2)],
        out_specs=(ospec, ospec, ospec),
        compiler_params=_cparams(("parallel",)),
        name=f"proj_mb_l{layer}",
    )(x2, mod_l, w_in, w_in, w_in)


def _bcast_left_end(w, c, row):
    n = w.shape[0]
    if c >= 8:
        parts = []
        for g in range(n // (2 * c)):
            r = g * 2 * c + c - 1
            parts.append(jnp.broadcast_to(w[r:r + 1, :], (2 * c, w.shape[1])))
        return parts[0] if len(parts) == 1 else jnp.concatenate(parts, axis=0)
    up = lambda x, s: pltpu.roll(x, n - s, 0)
    down = lambda x, s: pltpu.roll(x, s, 0)
    if c == 1:
        return jnp.where((row & 1) == 1, down(w, 1), w)
    if c == 2:
        z = jnp.where((row & 3) == 0, up(w, 1), w)
        return jnp.where((row & 3) >= 2, down(z, 2), z)
    z = jnp.where((row & 7) == 2, up(w, 1), w)
    z = jnp.where((row & 7) < 2, up(z, 2), z)
    return jnp.where((row & 7) >= 4, down(z, 4), z)


def _hgrn_kernel(q_ref, lf_ref, k_ref, v_ref, sg_ref, ng_ref, o_ref, st_scr, *, chunk, nchunk):
    @pl.when(pl.program_id(2) == 0)
    def _():
        st_scr[...] = jnp.zeros_like(st_scr)

    row = lax.broadcasted_iota(jnp.int32, (chunk, HEAD), 0)
    rr = lax.broadcasted_iota(jnp.int32, (chunk, chunk), 0)
    cc = lax.broadcasted_iota(jnp.int32, (chunk, chunk), 1)
    nlev = int(math.log2(chunk))
    ng = ng_ref[...]
    pair = [((rr >> (lev + 1)) == (cc >> (lev + 1))) & ((rr & (1 << lev)) != 0) & ((cc & (1 << lev)) == 0)
            for lev in range(nlev)]

    def body(ci, carry):
        rows = pl.ds(pl.multiple_of(ci * chunk, chunk), chunk)
        w = lf_ref[rows, :]
        qb = q_ref[rows, :]
        kb = k_ref[rows, :]
        q = qb.astype(F32)
        k = kb.astype(F32)
        v = v_ref[rows, :]
        attn = jnp.where(rr == cc, _dot_nt(qb, kb), 0.0)
        for lev in range(nlev):
            c = 1 << lev
            right = (row & c) != 0
            wl = _bcast_left_end(w, c, row)
            e = jnp.exp2(jnp.where(right, w, wl - w))
            p = _dot_nt((q * e).astype(BF16), (k * e).astype(BF16))
            attn = attn + jnp.where(pair[lev], p, 0.0)
            w = jnp.where(right, w + wl, w)
        bl = w[chunk - 1:chunk, :]
        qe = (q * jnp.exp2(w)).astype(BF16)
        ke = (k * jnp.exp2(bl - w)).astype(BF16)
        st = st_scr[...]
        o = _dot(attn.astype(BF16), v) + _dot_nt(qe, st.astype(BF16))
        st_scr[...] = st * jnp.exp2(bl) + _dot_tn(v, ke)
        o = o * lax.rsqrt(jnp.mean(o * o, axis=-1, keepdims=True) + RMS_EPS)
        o = o * ng * sg_ref[rows, :].astype(F32)
        o_ref[rows, :] = o.astype(BF16)
        return carry

    lax.fori_loop(0, nchunk, body, 0, unroll=min(HG_UNROLL, nchunk))


def _hgrn(qf, lf, kk, vv, sg, norm_g, *, layer, batch, seq):
    t, hgw = qf.shape
    heads = hgw // HEAD
    lb = min(512, seq)
    chunk = min(HG_CHUNK, lb)
    nsb = seq // lb
    spec = pl.BlockSpec((lb, HEAD), lambda b, h, s: (b * nsb + s, h))
    return pl.pallas_call(
        functools.partial(_hgrn_kernel, chunk=chunk, nchunk=lb // chunk),
        out_shape=jax.ShapeDtypeStruct((t, hgw), BF16),
        grid=(batch, heads, nsb),
        in_specs=[spec, spec, spec, spec, spec,
                  pl.BlockSpec((None, 1, HEAD), lambda b, h, s: (layer, 0, h))],
        out_specs=spec,
        scratch_shapes=[pltpu.VMEM((HEAD, HEAD), F32)],
        compiler_params=_cparams(("parallel", "parallel", "arbitrary")),
        name=f"hgrn_l{layer}",
    )(qf, lf, kk, vv, sg, norm_g.reshape(norm_g.shape[0], 1, hgw))


def _moba_kernel(q_ref, k_ref, v_ref, o_ref, kmh_scr, kml_scr, vt_scr, *, nblk, blk, nqb):
    it = pl.program_id(2)
    tq = nqb * blk
    nrow = kmh_scr.shape[0]

    @pl.when(it == 0)
    def _():
        kmh_scr[...] = jnp.zeros_like(kmh_scr)
        kml_scr[...] = jnp.zeros_like(kml_scr)
        for j in range(nblk):
            rows = slice(j * blk, (j + 1) * blk)
            km = jnp.sum(k_ref[rows, :].astype(F32), axis=0, keepdims=True) * (1.0 / blk)
            hi = km.astype(BF16)
            kmh_scr[j:j + 1, :] = hi
            kml_scr[j:j + 1, :] = (km - hi.astype(F32)).astype(BF16)
        for g in range(nblk // nqb):
            vt_scr[g] = v_ref[g * tq:(g + 1) * tq, :].astype(F32).T.astype(BF16)

    q = q_ref[...]
    scale = HEAD ** -0.5
    gate = _dot_nt(kmh_scr[...], q) + _dot_nt(kml_scr[...], q)
    kb = lax.broadcasted_iota(jnp.int32, (nrow, tq), 0).astype(F32)
    qlane = lax.broadcasted_iota(jnp.int32, (1, tq), 1)
    qb = (it * nqb + qlane // blk).astype(F32)
    g = jnp.where(kb < qb, gate, NEG)
    picks = []
    for _ in range(MB_TOPK):
        m = jnp.max(g, axis=0, keepdims=True)
        idx = jnp.min(jnp.where(g == m, kb, float(nrow)), axis=0, keepdims=True)
        picks.append(jnp.where(idx < qb, idx, -1.0))
        g = jnp.where(kb == idx, -3e38, g)

    def sel_bias(j_f):
        sel = (picks[0] == j_f) | (picks[1] == j_f) | (picks[2] == j_f)
        return jnp.where(sel, 0.0, NEG)

    def scores(g):
        rows = pl.ds(pl.multiple_of(g * tq, tq), tq)
        return _dot_nt(k_ref[rows, :], q) * scale

    def update(g, s, carry):
        m_prev, l_prev, acc = carry
        m_new = jnp.maximum(m_prev, jnp.max(s, axis=0, keepdims=True))
        a = jnp.exp(m_prev - m_new)
        p = jnp.exp(s - m_new)
        l_new = a * l_prev + jnp.sum(p, axis=0, keepdims=True)
        acc = a * acc + _dot(vt_scr[g], p.astype(BF16))
        return m_new, l_new, acc

    def past(g, carry):
        s = scores(g)
        j0 = (g * nqb).astype(F32)
        s = jnp.concatenate([s[d * blk:(d + 1) * blk, :] + sel_bias(j0 + d) for d in range(nqb)], axis=0)
        return update(g, s, carry)

    carry = (jnp.full((1, tq), NEG, F32), jnp.zeros((1, tq), F32), jnp.zeros((HEAD, tq), F32))
    carry = lax.fori_loop(0, it, past, carry)
    s = scores(it)
    j0 = (it * nqb).astype(F32)
    tri = (lax.broadcasted_iota(jnp.int32, (blk, blk), 0) <= lax.broadcasted_iota(jnp.int32, (blk, blk), 1))
    parts = []
    for d in range(nqb):
        sd = s[d * blk:(d + 1) * blk, :]
        biased = sd + sel_bias(j0 + d)
        lo, hi = d * blk, (d + 1) * blk
        cols = [biased[:, :lo]] if d else []
        cols.append(jnp.where(tri, sd[:, lo:hi], NEG))
        if hi < tq:
            cols.append(biased[:, hi:])
        parts.append(cols[0] if len(cols) == 1 else jnp.concatenate(cols, axis=1))
    s = parts[0] if nqb == 1 else jnp.concatenate(parts, axis=0)
    _, l_fin, acc = update(it, s, carry)
    o_ref[...] = (acc / l_fin).T.astype(BF16)


def _moba(mq, mk, mv, *, layer, batch, seq):
    t, mbw = mq.shape
    heads = mbw // HEAD
    blk = MB_BLOCK
    nblk = seq // blk
    nqb = math.gcd(nblk, MB_TILE_BLOCKS)
    assert seq % blk == 0
    nrow = -(-nblk // 16) * 16
    qspec = pl.BlockSpec((nqb * blk, HEAD), lambda b, h, i: (b * (nblk // nqb) + i, h))
    kvspec = pl.BlockSpec((seq, HEAD), lambda b, h, i: (b, h))
    return pl.pallas_call(
        functools.partial(_moba_kernel, nblk=nblk, blk=blk, nqb=nqb),
        out_shape=jax.ShapeDtypeStruct((t, mbw), BF16),
        grid=(batch, heads, nblk // nqb),
        in_specs=[qspec, kvspec, kvspec],
        out_specs=qspec,
        scratch_shapes=[pltpu.VMEM((nrow, HEAD), BF16), pltpu.VMEM((nrow, HEAD), BF16),
                        pltpu.VMEM((nblk // nqb, HEAD, nqb * blk), BF16)],
        compiler_params=_cparams(("parallel", "parallel", "arbitrary")),
        name=f"moba_l{layer}",
    )(mq, mk, mv)


def _merge_kernel(x_ref, mod_ref, oa_ref, ob_ref, wga_ref, wgb_ref, wa_ref, wb_ref, wo_ref,
                  lng_ref, lnb_ref, o_ref, *, alpha):
    shift, scale, gate = _mod_rows(mod_ref, 1)
    x = x_ref[...]
    h = (x * (1.0 + scale) + shift).astype(BF16)
    ya = _dot(oa_ref[...], wa_ref[...])
    merged = jax.nn.sigmoid(_dot(h, wga_ref[...])) * ya
    yb = _dot(ob_ref[...], wb_ref[...])
    merged = merged + jax.nn.sigmoid(_dot(h, wgb_ref[...])) * yb
    y = _dot(merged.astype(BF16), wo_ref[...])
    z = alpha * x + (1.0 + gate) * y
    o_ref[...] = _layer_norm(z, lng_ref[...], lnb_ref[...])


def _merge(x2, mod_l, oa, ob, w_gates, wa, wb, wo, lng, lnb, *, layer, seq, alpha):
    t, d = x2.shape
    hgw, mbw = oa.shape[1], ob.shape[1]
    tm = min(256, seq)
    spt = seq // tm
    one = pl.Buffered(1)
    return pl.pallas_call(
        functools.partial(_merge_kernel, alpha=alpha),
        out_shape=jax.ShapeDtypeStruct((t, d), F32),
        grid=(t // tm,),
        in_specs=[pl.BlockSpec((tm, d), lambda m: (m, 0)),
                  pl.BlockSpec((1, 3 * N_SUB, d), lambda m: (m // spt, 0, 0)),
                  pl.BlockSpec((tm, hgw), lambda m: (m, 0)),
                  pl.BlockSpec((tm, mbw), lambda m: (m, 0)),
                  pl.BlockSpec((None, d, d), lambda m: (layer, 0, 0), pipeline_mode=one),
                  pl.BlockSpec((None, d, d), lambda m: (layer, 0, 1), pipeline_mode=one),
                  pl.BlockSpec((None, hgw, d), lambda m: (layer, 0, 0), pipeline_mode=one),
                  pl.BlockSpec((None, mbw, d), lambda m: (layer, 0, 0), pipeline_mode=one),
                  pl.BlockSpec((None, d, d), lambda m: (layer, 0, 0), pipeline_mode=one),
                  pl.BlockSpec((1, d), lambda m: (0, 0)),
                  pl.BlockSpec((1, d), lambda m: (0, 0))],
        out_specs=pl.BlockSpec((tm, d), lambda m: (m, 0)),
        compiler_params=_cparams(("parallel",)),
        name=f"merge_l{layer}",
    )(x2, mod_l, oa, ob, w_gates, w_gates, wa, wb, wo, lng, lnb)


def kernel(x, c, ada_w, ada_b, ln_g, ln_b, ffn_w_gate, ffn_w_up, ffn_w_down, w_in, hg_lb_logits,
           hg_norm_g, w_branch_a, w_branch_b, w_out):
    batch, seq, d = x.shape
    depth = ada_w.shape[0]
    hgw = hg_norm_g.shape[1]
    mbw = w_branch_b.shape[1]
    assert w_in.shape[2] == 4 * hgw + 3 * mbw + 2 * d and hgw == mbw and d % hgw == 0
    alpha = (2 * depth) ** 0.25

    wg, wu, wd = ffn_w_gate.astype(BF16), ffn_w_up.astype(BF16), ffn_w_down.astype(BF16)
    nmix = 4 * hgw + 3 * mbw
    win, wgates = w_in[:, :, :nmix].astype(BF16), w_in[:, :, nmix:].astype(BF16)
    wa, wb, wo = w_branch_a.astype(BF16), w_branch_b.astype(BF16), w_out.astype(BF16)

    c_pad = jnp.pad(c, ((0, 8 - batch % 8 if batch % 8 else 0), (0, 0)))
    mod = _adaln(c_pad, ada_w, ada_b)[:, :batch].reshape(depth, batch, 3 * N_SUB, d)

    x2 = x.reshape(batch * seq, d)
    for l in range(depth):
        mod_l = mod[l]
        lng = [ln_g[l, j].reshape(1, d) for j in range(N_SUB)]
        lnb = [ln_b[l, j].reshape(1, d) for j in range(N_SUB)]
        x2 = _ffn(x2, mod_l, wg, wu, wd, lng[0], lnb[0], layer=l, which=0, sub=0, seq=seq, alpha=alpha)
        qf, lf, kk, vv, sg = _proj_hg(x2, mod_l, hg_lb_logits, win, layer=l, seq=seq, hgw=hgw)
        mq, mk, mv = _proj_mb(x2, mod_l, win, layer=l, seq=seq, mbw=mbw, col0=4 * hgw // mbw)
        oa = _hgrn(qf, lf, kk, vv, sg, hg_norm_g, layer=l, batch=batch, seq=seq)
        ob = _moba(mq, mk, mv, layer=l, batch=batch, seq=seq)
        x2 = _merge(x2, mod_l, oa, ob, wgates, wa, wb, wo, lng[1], lnb[1], layer=l, seq=seq, alpha=alpha)
        x2 = _ffn(x2, mod_l, wg, wu, wd, lng[2], lnb[2], layer=l, which=1, sub=2, seq=seq, alpha=alpha)
    return x2.reshape(batch, seq, d)
```

```python
import functools
import math

import jax
import jax.numpy as jnp
from jax import lax
from jax.experimental import pallas as pl
from jax.experimental.pallas import tpu as pltpu

F32 = jnp.float32
BF16 = jnp.bfloat16

HEAD = 128
MB_BLOCK = 256
MB_TOPK = 3
MB_TILE_BLOCKS = 4
MB_CHAIN_BLOCKS = 2
VT_PAD = 16
HG_CHUNK = 64
HG_UNROLL = 8
F_MIN = 1e-30
MACARON_W = 0.5
N_SUB = 3
LN_EPS = 1e-5
RMS_EPS = 1e-6
NEG = -1e30
LANES = 128
VMEM_LIMIT = 56 * 1024 * 1024


def _cparams(sem):
    return pltpu.CompilerParams(dimension_semantics=sem, vmem_limit_bytes=VMEM_LIMIT)


def _dot(a, b):
    return jnp.dot(a, b, preferred_element_type=F32)


def _dot_nt(a, b):
    return lax.dot_general(a, b, (((1,), (1,)), ((), ())), preferred_element_type=F32)


def _dot_tn(a, b):
    return lax.dot_general(a, b, (((0,), (0,)), ((), ())), preferred_element_type=F32)


def _silu(x):
    return x * jax.nn.sigmoid(x)


def _layer_norm(z, g, b):
    mu = jnp.mean(z, axis=-1, keepdims=True)
    zc = z - mu
    var = jnp.mean(zc * zc, axis=-1, keepdims=True)
    return zc * lax.rsqrt(var + LN_EPS) * g + b


def _mod_rows(mod_ref, sub):
    return (mod_ref[0, 3 * sub:3 * sub + 1, :], mod_ref[0, 3 * sub + 1:3 * sub + 2, :],
            mod_ref[0, 3 * sub + 2:3 * sub + 3, :])


def _adaln_kernel(c_ref, w_ref, b_ref, o_ref):
    cond = _silu(c_ref[...])
    o_ref[...] = _dot(cond, w_ref[...]) + b_ref[...]


def _adaln(c_pad, ada_w, ada_b):
    depth, d, n = ada_w.shape
    tn = min(1024, d)
    return pl.pallas_call(
        _adaln_kernel,
        out_shape=jax.ShapeDtypeStruct((depth, c_pad.shape[0], n), F32),
        grid=(depth, n // tn),
        in_specs=[pl.BlockSpec(c_pad.shape, lambda l, j: (0, 0)),
                  pl.BlockSpec((None, d, tn), lambda l, j: (l, 0, j)),
                  pl.BlockSpec((None, 1, tn), lambda l, j: (l, 0, j))],
        out_specs=pl.BlockSpec((None, c_pad.shape[0], tn), lambda l, j: (l, 0, j)),
        compiler_params=_cparams(("arbitrary", "arbitrary")),
        name="adaln",
    )(c_pad, ada_w, ada_b.reshape(depth, 1, n))


def _side_rows(r, nsteps):
    rb = 16 * -(-r // (16 * nsteps))
    return rb if r % rb == 0 else r


def _side_specs(side, nsteps, step):
    arrays, in_specs, out_specs, out_shapes = [], [], [], []
    for arr, lead in side:
        r, c = arr.shape[-2:]
        rb = _side_rows(r, nsteps)
        last = r // rb - 1
        arrays.append(arr)
        in_specs.append(pl.BlockSpec(
            (None,) * len(lead) + (rb, c),
            lambda *ids, lead=lead, last=last: lead + (jnp.minimum(step(*ids), last), 0)))
        out_specs.append(pl.BlockSpec((rb, c), lambda *ids, last=last: (jnp.minimum(step(*ids), last), 0)))
        out_shapes.append(jax.ShapeDtypeStruct((r, c), BF16))
    return arrays, in_specs, out_specs, out_shapes


def _cast_side(side_in, side_out):
    for src, dst in zip(side_in, side_out):
        dst[...] = src[...].astype(BF16)


def _ffn_kernel(*refs, sub, alpha, nf, nside):
    x_ref, mod_ref, wg_ref, wu_ref, wd_ref, lng_ref, lnb_ref = refs[:7]
    side_in = refs[7:7 + nside]
    o_ref = refs[7 + nside]
    side_out = refs[8 + nside:8 + 2 * nside]
    h_scr, acc_scr = refs[8 + 2 * nside:]
    f = pl.program_id(1)
    shift, scale, gate = _mod_rows(mod_ref, sub)

    @pl.when(f == 0)
    def _():
        h_scr[...] = (x_ref[...] * (1.0 + scale) + shift).astype(BF16)
        acc_scr[...] = jnp.zeros_like(acc_scr)

    _cast_side(side_in, side_out)

    h = h_scr[...]
    g = _dot(h, wg_ref[...])
    u = _dot(h, wu_ref[...])
    a = (_silu(g) * u).astype(BF16)
    acc_scr[...] += _dot(a, wd_ref[...])

    @pl.when(f == nf - 1)
    def _():
        z = alpha * x_ref[...] + MACARON_W * (1.0 + gate) * acc_scr[...]
        o_ref[...] = _layer_norm(z, lng_ref[...], lnb_ref[...])


def _ffn(x2, mod_l, wg, wu, wd, lng, lnb, side, *, layer, which, sub, seq, alpha):
    t, d = x2.shape
    dff = wg.shape[-1]
    tm = min(512, seq)
    tf = 512 if dff % 512 == 0 else dff
    nf = dff // tf
    spt = seq // tm
    side_arrays, side_in, side_out, side_shapes = _side_specs(side, (t // tm) * nf, lambda m, f: m * nf + f)
    kern = functools.partial(_ffn_kernel, sub=sub, alpha=alpha, nf=nf, nside=len(side))
    outs = pl.pallas_call(
        kern,
        out_shape=[jax.ShapeDtypeStruct((t, d), F32)] + side_shapes,
        grid=(t // tm, nf),
        in_specs=[pl.BlockSpec((tm, d), lambda m, f: (m, 0)),
                  pl.BlockSpec((1, 3 * N_SUB, d), lambda m, f: (m // spt, 0, 0)),
                  pl.BlockSpec((d, tf), lambda m, f: (0, f)),
                  pl.BlockSpec((d, tf), lambda m, f: (0, f)),
                  pl.BlockSpec((tf, d), lambda m, f: (f, 0)),
                  pl.BlockSpec((1, d), lambda m, f: (0, 0)),
                  pl.BlockSpec((1, d), lambda m, f: (0, 0))] + side_in,
        out_specs=[pl.BlockSpec((tm, d), lambda m, f: (m, 0))] + side_out,
        scratch_shapes=[pltpu.VMEM((tm, d), BF16), pltpu.VMEM((tm, d), F32)],
        compiler_params=_cparams(("arbitrary", "arbitrary")),
        name=f"ffn_l{layer}_{which}",
    )(x2, mod_l, wg, wu, wd, lng, lnb, *side_arrays)
    return outs[0], outs[1:]


def _proj_hg_kernel(*refs, layer, nside):
    x_ref, mod_ref, lbl_ref, wq_ref, wf_ref, wi_ref, wg_ref = refs[:7]
    q_ref, lf_ref, k_ref, v_ref, sg_ref = refs[7 + nside:12 + nside]
    _cast_side(refs[7:7 + nside], refs[12 + nside:])
    shift, scale, _ = _mod_rows(mod_ref, 1)
    h = (x_ref[...] * (1.0 + scale) + shift).astype(BF16)
    logits = lbl_ref[...]
    e = jnp.exp(logits - jnp.max(logits, axis=0, keepdims=True))
    p = e / jnp.sum(e, axis=0, keepdims=True)
    lb = jnp.zeros_like(p[0:1, :])
    for i in range(1, layer + 1):
        lb = lb + p[i:i + 1, :]
    q_ref[...] = _silu(_dot(h, wq_ref[...])).astype(BF16)
    z = _dot(h, wf_ref[...])
    fgate = lb + (1.0 - lb) * jax.nn.sigmoid(z)
    lf_ref[...] = jnp.log2(jnp.maximum(fgate, F_MIN))
    k_ref[...] = ((1.0 - lb) * jax.nn.sigmoid(-z)).astype(BF16)
    v_ref[...] = _dot(h, wi_ref[...]).astype(BF16)
    sg_ref[...] = _silu(_dot(h, wg_ref[...])).astype(BF16)


def _proj_hg(x2, mod_l, lb_logits, w_in, side, *, layer, seq, hgw):
    t, d = x2.shape
    tm = min(512, seq)
    spt = seq // tm
    depth = lb_logits.shape[0]
    side_arrays, side_in, side_out, side_shapes = _side_specs(side, t // tm, lambda m: m)

    def wspec(col):
        return pl.BlockSpec((d, hgw), lambda m, col=col: (0, col), pipeline_mode=pl.Buffered(1))

    out = jax.ShapeDtypeStruct((t, hgw), BF16)
    ospec = pl.BlockSpec((tm, hgw), lambda m: (m, 0))
    outs = pl.pallas_call(
        functools.partial(_proj_hg_kernel, layer=layer, nside=len(side)),
        out_shape=[out, jax.ShapeDtypeStruct((t, hgw), F32), out, out, out] + side_shapes,
        grid=(t // tm,),
        in_specs=[pl.BlockSpec((tm, d), lambda m: (m, 0)),
                  pl.BlockSpec((1, 3 * N_SUB, d), lambda m: (m // spt, 0, 0)),
                  pl.BlockSpec((depth, hgw), lambda m: (0, 0)),
                  wspec(0), wspec(1), wspec(2), wspec(3)] + side_in,
        out_specs=[ospec, ospec, ospec, ospec, ospec] + side_out,
        compiler_params=_cparams(("arbitrary",)),
        name=f"proj_hg_l{layer}",
    )(x2, mod_l, lb_logits, w_in, w_in, w_in, w_in, *side_arrays)
    return outs[:5], outs[5:]


def _proj_mb_kernel(*refs, nside):
    x_ref, mod_ref, wq_ref, wk_ref, wv_ref = refs[:5]
    q_ref, k_ref, v_ref = refs[5 + nside:8 + nside]
    _cast_side(refs[5:5 + nside], refs[8 + nside:])
    shift, scale, _ = _mod_rows(mod_ref, 1)
    h = (x_ref[...] * (1.0 + scale) + shift).astype(BF16)
    q_ref[...] = _dot(h, wq_ref[...]).astype(BF16)
    k_ref[...] = _dot(h, wk_ref[...]).astype(BF16)
    v_ref[...] = _dot(h, wv_ref[...]).astype(BF16)


def _proj_mb(x2, mod_l, w_in, side, *, layer, seq, mbw, col0):
    t, d = x2.shape
    tm = min(512, seq)
    spt = seq // tm
    side_arrays, side_in, side_out, side_shapes = _side_specs(side, t // tm, lambda m: m)

    def wspec(col):
        return pl.BlockSpec((d, mbw), lambda m, col=col: (0, col), pipeline_mode=pl.Buffered(1))

    out = jax.ShapeDtypeStruct((t, mbw), BF16)
    ospec = pl.BlockSpec((tm, mbw), lambda m: (m, 0))
    outs = pl.pallas_call(
        functools.partial(_proj_mb_kernel, nside=len(side)),
        out_shape=[out, out, out] + side_shapes,
        grid=(t // tm,),
        in_specs=[pl.BlockSpec((tm, d), lambda m: (m, 0)),
                  pl.BlockSpec((1, 3 * N_SUB, d), lambda m: (m // spt, 0, 0)),
                  wspec(col0), wspec(col0 + 1), wspec(col0 + 2)] + side_in,
        out_specs=[ospec, ospec, ospec] + side_out,
        compiler_params=_cparams(("arbitrary",)),
        name=f"proj_mb_l{layer}",
    )(x2, mod_l, w_in, w_in, w_in, *side_arrays)
    return outs[:3], outs[3:]


def _bcast_left_end(w, c, row):
    n = w.shape[0]
    if c >= 8:
        parts = []
        for g in range(n // (2 * c)):
            r = g * 2 * c + c - 1
            parts.append(jnp.broadcast_to(w[r:r + 1, :], (2 * c, w.shape[1])))
        return parts[0] if len(parts) == 1 else jnp.concatenate(parts, axis=0)
    up = lambda x, s: pltpu.roll(x, n - s, 0)
    down = lambda x, s: pltpu.roll(x, s, 0)
    if c == 1:
        return jnp.where((row & 1) == 1, down(w, 1), w)
    if c == 2:
        z = jnp.where((row & 3) == 0, up(w, 1), w)
        return jnp.where((row & 3) >= 2, down(z, 2), z)
    z = jnp.where((row & 7) == 2, up(w, 1), w)
    z = jnp.where((row & 7) < 2, up(z, 2), z)
    return jnp.where((row & 7) >= 4, down(z, 4), z)


def _hgrn_kernel(q_ref, lf_ref, k_ref, v_ref, sg_ref, ng_ref, o_ref, st_scr, *, chunk, nchunk):
    @pl.when(pl.program_id(2) == 0)
    def _():
        st_scr[...] = jnp.zeros_like(st_scr)

    row = lax.broadcasted_iota(jnp.int32, (chunk, HEAD), 0)
    rr = lax.broadcasted_iota(jnp.int32, (chunk, chunk), 0)
    cc = lax.broadcasted_iota(jnp.int32, (chunk, chunk), 1)
    nlev = int(math.log2(chunk))
    ng = ng_ref[...]
    pair = [((rr >> (lev + 1)) == (cc >> (lev + 1))) & ((rr & (1 << lev)) != 0) & ((cc & (1 << lev)) == 0)
            for lev in range(nlev)]

    def body(ci, carry):
        rows = pl.ds(pl.multiple_of(ci * chunk, chunk), chunk)
        w = lf_ref[rows, :]
        qb = q_ref[rows, :]
        kb = k_ref[rows, :]
        q = qb.astype(F32)
        k = kb.astype(F32)
        v = v_ref[rows, :]
        attn = jnp.where(rr == cc, _dot_nt(qb, kb), 0.0)
        for lev in range(nlev):
            c = 1 << lev
            right = (row & c) != 0
            wl = _bcast_left_end(w, c, row)
            e = jnp.exp2(jnp.where(right, w, wl - w))
            p = _dot_nt((q * e).astype(BF16), (k * e).astype(BF16))
            attn = attn + jnp.where(pair[lev], p, 0.0)
            w = jnp.where(right, w + wl, w)
        bl = w[chunk - 1:chunk, :]
        qe = (q * jnp.exp2(w)).astype(BF16)
        ke = (k * jnp.exp2(bl - w)).astype(BF16)
        st = st_scr[...]
        o = _dot(attn.astype(BF16), v) + _dot_nt(qe, st.astype(BF16))
        st_scr[...] = st * jnp.exp2(bl) + _dot_tn(v, ke)
        o = o * lax.rsqrt(jnp.mean(o * o, axis=-1, keepdims=True) + RMS_EPS)
        o = o * ng * sg_ref[rows, :].astype(F32)
        o_ref[rows, :] = o.astype(BF16)
        return carry

    lax.fori_loop(0, nchunk, body, 0, unroll=min(HG_UNROLL, nchunk))


def _hgrn(qf, lf, kk, vv, sg, norm_g, *, layer, batch, seq):
    t, hgw = qf.shape
    heads = hgw // HEAD
    lb = min(512, seq)
    chunk = min(HG_CHUNK, lb)
    nsb = seq // lb
    spec = pl.BlockSpec((lb, HEAD), lambda b, h, s: (b * nsb + s, h))
    return pl.pallas_call(
        functools.partial(_hgrn_kernel, chunk=chunk, nchunk=lb // chunk),
        out_shape=jax.ShapeDtypeStruct((t, hgw), BF16),
        grid=(batch, heads, nsb),
        in_specs=[spec, spec, spec, spec, spec,
                  pl.BlockSpec((None, 1, HEAD), lambda b, h, s: (layer, 0, h))],
        out_specs=spec,
        scratch_shapes=[pltpu.VMEM((HEAD, HEAD), F32)],
        compiler_params=_cparams(("parallel", "parallel", "arbitrary")),
        name=f"hgrn_l{layer}",
    )(qf, lf, kk, vv, sg, norm_g.reshape(norm_g.shape[0], 1, hgw))


def _moba_kernel(q_ref, k_ref, v_ref, o_ref, kmh_scr, kml_scr, vt_scr, *, nblk, blk, nqb):
    it = pl.program_id(2)
    tq = nqb * blk
    nrow = kmh_scr.shape[0]

    @pl.when(it == 0)
    def _():
        kmh_scr[...] = jnp.zeros_like(kmh_scr)
        kml_scr[...] = jnp.zeros_like(kml_scr)
        for j in range(nblk):
            rows = slice(j * blk, (j + 1) * blk)
            km = jnp.sum(k_ref[rows, :].astype(F32), axis=0, keepdims=True) * (1.0 / blk)
            hi = km.astype(BF16)
            kmh_scr[j:j + 1, :] = hi
            kml_scr[j:j + 1, :] = (km - hi.astype(F32)).astype(BF16)
        ones_rows = (lax.broadcasted_iota(jnp.int32, (VT_PAD, tq), 0) == 0).astype(BF16)
        for g in range(nblk // nqb):
            vt_scr[g, :HEAD, :] = v_ref[g * tq:(g + 1) * tq, :].astype(F32).T.astype(BF16)
            vt_scr[g, HEAD:, :] = ones_rows

    q = q_ref[...]
    scale = HEAD ** -0.5
    gate = _dot_nt(kmh_scr[...], q) + _dot_nt(kml_scr[...], q)
    kb = lax.broadcasted_iota(jnp.int32, (nrow, tq), 0).astype(F32)
    qlane = lax.broadcasted_iota(jnp.int32, (1, tq), 1)
    qb = (it * nqb + qlane // blk).astype(F32)
    g = jnp.where(kb < qb, gate, NEG)
    picks = []
    for _ in range(MB_TOPK):
        m = jnp.max(g, axis=0, keepdims=True)
        idx = jnp.min(jnp.where(g == m, kb, float(nrow)), axis=0, keepdims=True)
        picks.append(jnp.where(idx < qb, idx, -1.0))
        g = jnp.where(kb == idx, -3e38, g)

    c = scale * math.log2(math.e)
    tri = (lax.broadcasted_iota(jnp.int32, (blk, blk), 0) <= lax.broadcasted_iota(jnp.int32, (blk, blk), 1))

    ncb = math.gcd(nqb, MB_CHAIN_BLOCKS)
    cw = ncb * blk

    def scores(ch, g, own):
        e0 = ch * ncb
        nkb = e0 + ncb if own else nqb
        q_c = q[e0 * blk:(e0 + ncb) * blk, :]
        rows = pl.ds(pl.multiple_of(g * tq, tq), nkb * blk)
        s = _dot_nt(k_ref[rows, :], q_c)
        j0 = (g * nqb).astype(F32)
        parts = []
        for d in range(nkb):
            sd = s[d * blk:(d + 1) * blk, :]
            j_f = j0 + d
            cols = []
            for e in range(e0, e0 + ncb):
                sde = sd[:, (e - e0) * blk:(e - e0 + 1) * blk]
                lanes = slice(e * blk, (e + 1) * blk)
                if own and d == e:
                    cols.append(jnp.where(tri, sde, NEG))
                elif own and d > e:
                    cols.append(jnp.full_like(sde, NEG))
                else:
                    sel = (picks[0][:, lanes] == j_f) | (picks[1][:, lanes] == j_f) | (picks[2][:, lanes] == j_f)
                    cols.append(sde + jnp.where(sel, 0.0, NEG))
            parts.append(cols[0] if ncb == 1 else jnp.concatenate(cols, axis=1))
        return parts[0] if nkb == 1 else jnp.concatenate(parts, axis=0)

    def update(g, s, carry):
        m_prev, acc = carry
        m_new = jnp.maximum(m_prev, jnp.max(s, axis=0, keepdims=True))
        a = jnp.exp2((m_prev - m_new) * c)
        p = jnp.exp2((s - m_new) * c)
        acc = a * acc + _dot(vt_scr[g, :, :s.shape[0]], p.astype(BF16))
        return m_new, acc

    nch = nqb // ncb

    def step(g, own, carry):
        ss = [scores(ch, g, own) for ch in range(nch)]
        return tuple(update(g, ss[ch], carry[ch]) for ch in range(nch))

    def past(g, carry):
        return step(g, False, carry)

    init = (jnp.full((1, cw), NEG, F32), jnp.zeros((HEAD + VT_PAD, cw), F32))
    carry = lax.fori_loop(0, it, past, tuple(init for _ in range(nch)))
    carry = step(it, True, carry)
    outs = [acc[:HEAD, :] / acc[HEAD:HEAD + 1, :] for _, acc in carry]
    out = outs[0] if nch == 1 else jnp.concatenate(outs, axis=1)
    o_ref[...] = out.T.astype(BF16)


def _moba(mq, mk, mv, *, layer, batch, seq):
    t, mbw = mq.shape
    heads = mbw // HEAD
    blk = MB_BLOCK
    nblk = seq // blk
    nqb = math.gcd(nblk, MB_TILE_BLOCKS)
    assert seq % blk == 0
    nrow = -(-nblk // 16) * 16
    qspec = pl.BlockSpec((nqb * blk, HEAD), lambda b, h, i: (b * (nblk // nqb) + i, h))
    kvspec = pl.BlockSpec((seq, HEAD), lambda b, h, i: (b, h))
    return pl.pallas_call(
        functools.partial(_moba_kernel, nblk=nblk, blk=blk, nqb=nqb),
        out_shape=jax.ShapeDtypeStruct((t, mbw), BF16),
        grid=(batch, heads, nblk // nqb),
        in_specs=[qspec, kvspec, kvspec],
        out_specs=qspec,
        scratch_shapes=[pltpu.VMEM((nrow, HEAD), BF16), pltpu.VMEM((nrow, HEAD), BF16),
                        pltpu.VMEM((nblk // nqb, HEAD + VT_PAD, nqb * blk), BF16)],
        compiler_params=_cparams(("parallel", "parallel", "arbitrary")),
        name=f"moba_l{layer}",
    )(mq, mk, mv)


def _merge_kernel(*refs, alpha, ng):
    x_ref, mod_ref, oa_ref, ob_ref = refs[:4]
    wga_refs, wgb_refs = refs[4:4 + ng], refs[4 + ng:4 + 2 * ng]
    wa_ref, wb_ref, wo_ref, lng_ref, lnb_ref, o_ref = refs[4 + 2 * ng:]
    shift, scale, gate = _mod_rows(mod_ref, 1)
    x = x_ref[...]
    h = (x * (1.0 + scale) + shift).astype(BF16)

    def gates(w_refs):
        parts = [jax.nn.sigmoid(_dot(h, w[...])) for w in w_refs]
        return parts[0] if ng == 1 else jnp.concatenate(parts, axis=1)

    merged = gates(wga_refs) * _dot(oa_ref[...], wa_ref[...])
    merged = merged + gates(wgb_refs) * _dot(ob_ref[...], wb_ref[...])
    y = _dot(merged.astype(BF16), wo_ref[...])
    z = alpha * x + (1.0 + gate) * y
    o_ref[...] = _layer_norm(z, lng_ref[...], lnb_ref[...])


def _merge(x2, mod_l, oa, ob, w_in, wa, wb, wo, lng, lnb, *, layer, seq, alpha):
    t, d = x2.shape
    hgw, mbw = oa.shape[1], ob.shape[1]
    tm = min(256, seq)
    spt = seq // tm
    one = pl.Buffered(1)
    gw = math.gcd(d, w_in.shape[1] - 2 * d)
    ng = d // gw
    g0 = (w_in.shape[1] - 2 * d) // gw
    gspecs = [pl.BlockSpec((d, gw), lambda m, col=g0 + i: (0, col), pipeline_mode=one) for i in range(2 * ng)]
    return pl.pallas_call(
        functools.partial(_merge_kernel, alpha=alpha, ng=ng),
        out_shape=jax.ShapeDtypeStruct((t, d), F32),
        grid=(t // tm,),
        in_specs=[pl.BlockSpec((tm, d), lambda m: (m, 0)),
                  pl.BlockSpec((1, 3 * N_SUB, d), lambda m: (m // spt, 0, 0)),
                  pl.BlockSpec((tm, hgw), lambda m: (m, 0)),
                  pl.BlockSpec((tm, mbw), lambda m: (m, 0))] + gspecs + [
                  pl.BlockSpec((hgw, d), lambda m: (0, 0), pipeline_mode=one),
                  pl.BlockSpec((mbw, d), lambda m: (0, 0), pipeline_mode=one),
                  pl.BlockSpec((d, d), lambda m: (0, 0), pipeline_mode=one),
                  pl.BlockSpec((1, d), lambda m: (0, 0)),
                  pl.BlockSpec((1, d), lambda m: (0, 0))],
        out_specs=pl.BlockSpec((tm, d), lambda m: (m, 0)),
        compiler_params=_cparams(("parallel",)),
        name=f"merge_l{layer}",
    )(x2, mod_l, oa, ob, *([w_in] * (2 * ng)), wa, wb, wo, lng, lnb)


def kernel(x, c, ada_w, ada_b, ln_g, ln_b, ffn_w_gate, ffn_w_up, ffn_w_down, w_in, hg_lb_logits,
           hg_norm_g, w_branch_a, w_branch_b, w_out):
    batch, seq, d = x.shape
    depth = ada_w.shape[0]
    hgw = hg_norm_g.shape[1]
    mbw = w_branch_b.shape[1]
    assert w_in.shape[2] == 4 * hgw + 3 * mbw + 2 * d and hgw == mbw and d % hgw == 0
    alpha = (2 * depth) ** 0.25

    c_pad = jnp.pad(c, ((0, 8 - batch % 8 if batch % 8 else 0), (0, 0)))
    mod = _adaln(c_pad, ada_w, ada_b)[:, :batch].reshape(depth, batch, 3 * N_SUB, d)

    def ffn_w(l, j):
        return [(ffn_w_gate, (l, j)), (ffn_w_up, (l, j)), (ffn_w_down, (l, j))]

    def branch_w(l):
        return [(w_branch_a, (l,)), (w_branch_b, (l,)), (w_out, (l,))]

    ffn0 = [w[0, 0].astype(BF16) for w in (ffn_w_gate, ffn_w_up, ffn_w_down)]
    win = w_in[0].astype(BF16)
    wa, wb, wo = [w[0].astype(BF16) for w in (w_branch_a, w_branch_b, w_out)]

    x2 = x.reshape(batch * seq, d)
    for l in range(depth):
        more = l + 1 < depth
        mod_l = mod[l]
        lng = [ln_g[l, j].reshape(1, d) for j in range(N_SUB)]
        lnb = [ln_b[l, j].reshape(1, d) for j in range(N_SUB)]
        x2, _ = _ffn(x2, mod_l, *ffn0, lng[0], lnb[0], [], layer=l, which=0, sub=0, seq=seq, alpha=alpha)
        (qf, lf, kk, vv, sg), cast_h = _proj_hg(x2, mod_l, hg_lb_logits, win,
                                               ffn_w(l, 1) + (branch_w(l + 1) if more else []),
                                               layer=l, seq=seq, hgw=hgw)
        (mq, mk, mv), cast_m = _proj_mb(x2, mod_l, win, ffn_w(l + 1, 0) if more else [],
                                        layer=l, seq=seq, mbw=mbw, col0=4 * hgw // mbw)
        oa = _hgrn(qf, lf, kk, vv, sg, hg_norm_g, layer=l, batch=batch, seq=seq)
        ob = _moba(mq, mk, mv, layer=l, batch=batch, seq=seq)
        x2 = _merge(x2, mod_l, oa, ob, win, wa, wb, wo, lng[1], lnb[1], layer=l, seq=seq, alpha=alpha)
        x2, cast_f = _ffn(x2, mod_l, *cast_h[:3], lng[2], lnb[2], [(w_in, (l + 1,))] if more else [],
                          layer=l, which=1, sub=2, seq=seq, alpha=alpha)
        if more:
            ffn0, (wa, wb, wo), (win,) = cast_m, cast_h[3:], cast_f
    return x2.reshape(batch, seq, d)
```

```python
import functools
import math

import jax
import jax.numpy as jnp
from jax import lax
from jax.experimental import pallas as pl
from jax.experimental.pallas import tpu as pltpu

F32 = jnp.float32
BF16 = jnp.bfloat16

HEAD = 128
MB_BLOCK = 256
MB_TOPK = 3
MB_TILE_BLOCKS = 4
MB_CHAIN_BLOCKS = 2
VT_PAD = 16
HG_CHUNK = 64
HG_UNROLL = 8
HG_HEADS_PER_STEP = 2
F_MIN = 1e-30
MACARON_W = 0.5
N_SUB = 3
LN_EPS = 1e-5
RMS_EPS = 1e-6
NEG = -1e30
LANES = 128
VMEM_LIMIT = 56 * 1024 * 1024


def _cparams(sem):
    return pltpu.CompilerParams(dimension_semantics=sem, vmem_limit_bytes=VMEM_LIMIT)


def _dot(a, b):
    return jnp.dot(a, b, preferred_element_type=F32)


def _dot_nt(a, b):
    return lax.dot_general(a, b, (((1,), (1,)), ((), ())), preferred_element_type=F32)


def _dot_tn(a, b):
    return lax.dot_general(a, b, (((0,), (0,)), ((), ())), preferred_element_type=F32)


def _silu(x):
    return x * jax.nn.sigmoid(x)


def _layer_norm(z, g, b):
    mu = jnp.mean(z, axis=-1, keepdims=True)
    zc = z - mu
    var = jnp.mean(zc * zc, axis=-1, keepdims=True)
    return zc * lax.rsqrt(var + LN_EPS) * g + b


def _mod_rows(mod_ref, sub):
    return (mod_ref[0, 3 * sub:3 * sub + 1, :], mod_ref[0, 3 * sub + 1:3 * sub + 2, :],
            mod_ref[0, 3 * sub + 2:3 * sub + 3, :])


def _adaln_kernel(c_ref, w_ref, b_ref, o_ref):
    cond = _silu(c_ref[...])
    o_ref[...] = _dot(cond, w_ref[...]) + b_ref[...]


def _adaln(c_pad, ada_w, ada_b):
    depth, d, n = ada_w.shape
    tn = min(1024, d)
    return pl.pallas_call(
        _adaln_kernel,
        out_shape=jax.ShapeDtypeStruct((depth, c_pad.shape[0], n), F32),
        grid=(depth, n // tn),
        in_specs=[pl.BlockSpec(c_pad.shape, lambda l, j: (0, 0)),
                  pl.BlockSpec((None, d, tn), lambda l, j: (l, 0, j)),
                  pl.BlockSpec((None, 1, tn), lambda l, j: (l, 0, j))],
        out_specs=pl.BlockSpec((None, c_pad.shape[0], tn), lambda l, j: (l, 0, j)),
        compiler_params=_cparams(("arbitrary", "arbitrary")),
        name="adaln",
    )(c_pad, ada_w, ada_b.reshape(depth, 1, n))


def _side_rows(r, nsteps):
    rb = 16 * -(-r // (16 * nsteps))
    return rb if r % rb == 0 else r


def _side_specs(side, nsteps, step):
    arrays, in_specs, out_specs, out_shapes = [], [], [], []
    for arr, lead in side:
        r, c = arr.shape[-2:]
        rb = _side_rows(r, nsteps)
        last = r // rb - 1
        arrays.append(arr)
        in_specs.append(pl.BlockSpec(
            (None,) * len(lead) + (rb, c),
            lambda *ids, lead=lead, last=last: lead + (jnp.minimum(step(*ids), last), 0)))
        out_specs.append(pl.BlockSpec((rb, c), lambda *ids, last=last: (jnp.minimum(step(*ids), last), 0)))
        out_shapes.append(jax.ShapeDtypeStruct((r, c), BF16))
    return arrays, in_specs, out_specs, out_shapes


def _cast_side(side_in, side_out):
    for src, dst in zip(side_in, side_out):
        dst[...] = src[...].astype(BF16)


def _ffn_kernel(*refs, sub, alpha, nf, nside):
    x_ref, mod_ref, wg_ref, wu_ref, wd_ref, lng_ref, lnb_ref = refs[:7]
    side_in = refs[7:7 + nside]
    o_ref = refs[7 + nside]
    side_out = refs[8 + nside:8 + 2 * nside]
    h_scr, acc_scr = refs[8 + 2 * nside:]
    f = pl.program_id(1)
    shift, scale, gate = _mod_rows(mod_ref, sub)

    @pl.when(f == 0)
    def _():
        h_scr[...] = (x_ref[...] * (1.0 + scale) + shift).astype(BF16)
        acc_scr[...] = jnp.zeros_like(acc_scr)

    _cast_side(side_in, side_out)

    h = h_scr[...]
    g = _dot(h, wg_ref[...])
    u = _dot(h, wu_ref[...])
    a = (_silu(g) * u).astype(BF16)
    acc_scr[...] += _dot(a, wd_ref[...])

    @pl.when(f == nf - 1)
    def _():
        z = alpha * x_ref[...] + MACARON_W * (1.0 + gate) * acc_scr[...]
        o_ref[...] = _layer_norm(z, lng_ref[...], lnb_ref[...])


def _ffn(x2, mod_l, wg, wu, wd, lng, lnb, side, *, layer, which, sub, seq, alpha):
    t, d = x2.shape
    dff = wg.shape[-1]
    tm = min(512, seq)
    tf = 512 if dff % 512 == 0 else dff
    nf = dff // tf
    spt = seq // tm
    side_arrays, side_in, side_out, side_shapes = _side_specs(side, (t // tm) * nf, lambda m, f: m * nf + f)
    kern = functools.partial(_ffn_kernel, sub=sub, alpha=alpha, nf=nf, nside=len(side))
    outs = pl.pallas_call(
        kern,
        out_shape=[jax.ShapeDtypeStruct((t, d), F32)] + side_shapes,
        grid=(t // tm, nf),
        in_specs=[pl.BlockSpec((tm, d), lambda m, f: (m, 0)),
                  pl.BlockSpec((1, 3 * N_SUB, d), lambda m, f: (m // spt, 0, 0)),
                  pl.BlockSpec((d, tf), lambda m, f: (0, f)),
                  pl.BlockSpec((d, tf), lambda m, f: (0, f)),
                  pl.BlockSpec((tf, d), lambda m, f: (f, 0)),
                  pl.BlockSpec((1, d), lambda m, f: (0, 0)),
                  pl.BlockSpec((1, d), lambda m, f: (0, 0))] + side_in,
        out_specs=[pl.BlockSpec((tm, d), lambda m, f: (m, 0))] + side_out,
        scratch_shapes=[pltpu.VMEM((tm, d), BF16), pltpu.VMEM((tm, d), F32)],
        compiler_params=_cparams(("arbitrary", "arbitrary")),
        name=f"ffn_l{layer}_{which}",
    )(x2, mod_l, wg, wu, wd, lng, lnb, *side_arrays)
    return outs[0], outs[1:]


def _proj_hg_kernel(*refs, layer, nside):
    x_ref, mod_ref, lbl_ref, wq_ref, wf_ref, wi_ref, wg_ref = refs[:7]
    q_ref, lf_ref, k_ref, v_ref, sg_ref = refs[7 + nside:12 + nside]
    _cast_side(refs[7:7 + nside], refs[12 + nside:])
    shift, scale, _ = _mod_rows(mod_ref, 1)
    h = (x_ref[...] * (1.0 + scale) + shift).astype(BF16)
    logits = lbl_ref[...]
    e = jnp.exp(logits - jnp.max(logits, axis=0, keepdims=True))
    p = e / jnp.sum(e, axis=0, keepdims=True)
    lb = jnp.zeros_like(p[0:1, :])
    for i in range(1, layer + 1):
        lb = lb + p[i:i + 1, :]
    q_ref[...] = _silu(_dot(h, wq_ref[...])).astype(BF16)
    z = _dot(h, wf_ref[...])
    fgate = lb + (1.0 - lb) * jax.nn.sigmoid(z)
    lf_ref[...] = jnp.log2(jnp.maximum(fgate, F_MIN))
    k_ref[...] = ((1.0 - lb) * jax.nn.sigmoid(-z)).astype(BF16)
    v_ref[...] = _dot(h, wi_ref[...]).astype(BF16)
    sg_ref[...] = _silu(_dot(h, wg_ref[...])).astype(BF16)


def _proj_hg(x2, mod_l, lb_logits, w_in, side, *, layer, seq, hgw):
    t, d = x2.shape
    tm = min(512, seq)
    spt = seq // tm
    depth = lb_logits.shape[0]
    side_arrays, side_in, side_out, side_shapes = _side_specs(side, t // tm, lambda m: m)

    def wspec(col):
        return pl.BlockSpec((d, hgw), lambda m, col=col: (0, col), pipeline_mode=pl.Buffered(1))

    out = jax.ShapeDtypeStruct((t, hgw), BF16)
    ospec = pl.BlockSpec((tm, hgw), lambda m: (m, 0))
    outs = pl.pallas_call(
        functools.partial(_proj_hg_kernel, layer=layer, nside=len(side)),
        out_shape=[out, jax.ShapeDtypeStruct((t, hgw), F32), out, out, out] + side_shapes,
        grid=(t // tm,),
        in_specs=[pl.BlockSpec((tm, d), lambda m: (m, 0)),
                  pl.BlockSpec((1, 3 * N_SUB, d), lambda m: (m // spt, 0, 0)),
                  pl.BlockSpec((depth, hgw), lambda m: (0, 0)),
                  wspec(0), wspec(1), wspec(2), wspec(3)] + side_in,
        out_specs=[ospec, ospec, ospec, ospec, ospec] + side_out,
        compiler_params=_cparams(("arbitrary",)),
        name=f"proj_hg_l{layer}",
    )(x2, mod_l, lb_logits, w_in, w_in, w_in, w_in, *side_arrays)
    return outs[:5], outs[5:]


def _proj_mb_kernel(*refs, nside):
    x_ref, mod_ref, wq_ref, wk_ref, wv_ref = refs[:5]
    q_ref, k_ref, v_ref = refs[5 + nside:8 + nside]
    _cast_side(refs[5:5 + nside], refs[8 + nside:])
    shift, scale, _ = _mod_rows(mod_ref, 1)
    h = (x_ref[...] * (1.0 + scale) + shift).astype(BF16)
    q_ref[...] = _dot(h, wq_ref[...]).astype(BF16)
    k_ref[...] = _dot(h, wk_ref[...]).astype(BF16)
    v_ref[...] = _dot(h, wv_ref[...]).astype(BF16)


def _proj_mb(x2, mod_l, w_in, side, *, layer, seq, mbw, col0):
    t, d = x2.shape
    tm = min(512, seq)
    spt = seq // tm
    side_arrays, side_in, side_out, side_shapes = _side_specs(side, t // tm, lambda m: m)

    def wspec(col):
        return pl.BlockSpec((d, mbw), lambda m, col=col: (0, col), pipeline_mode=pl.Buffered(1))

    out = jax.ShapeDtypeStruct((t, mbw), BF16)
    ospec = pl.BlockSpec((tm, mbw), lambda m: (m, 0))
    outs = pl.pallas_call(
        functools.partial(_proj_mb_kernel, nside=len(side)),
        out_shape=[out, out, out] + side_shapes,
        grid=(t // tm,),
        in_specs=[pl.BlockSpec((tm, d), lambda m: (m, 0)),
                  pl.BlockSpec((1, 3 * N_SUB, d), lambda m: (m // spt, 0, 0)),
                  wspec(col0), wspec(col0 + 1), wspec(col0 + 2)] + side_in,
        out_specs=[ospec, ospec, ospec] + side_out,
        compiler_params=_cparams(("arbitrary",)),
        name=f"proj_mb_l{layer}",
    )(x2, mod_l, w_in, w_in, w_in, *side_arrays)
    return outs[:3], outs[3:]


def _bcast_left_end(w, c, row):
    n = w.shape[0]
    if c >= 8:
        parts = []
        for g in range(n // (2 * c)):
            r = g * 2 * c + c - 1
            parts.append(jnp.broadcast_to(w[r:r + 1, :], (2 * c, w.shape[1])))
        return parts[0] if len(parts) == 1 else jnp.concatenate(parts, axis=0)
    up = lambda x, s: pltpu.roll(x, n - s, 0)
    down = lambda x, s: pltpu.roll(x, s, 0)
    if c == 1:
        return jnp.where((row & 1) == 1, down(w, 1), w)
    if c == 2:
        z = jnp.where((row & 3) == 0, up(w, 1), w)
        return jnp.where((row & 3) >= 2, down(z, 2), z)
    z = jnp.where((row & 7) == 2, up(w, 1), w)
    z = jnp.where((row & 7) < 2, up(z, 2), z)
    return jnp.where((row & 7) >= 4, down(z, 4), z)


def _hgrn_kernel(q_ref, lf_ref, k_ref, v_ref, sg_ref, ng_ref, o_ref, st_scr, *, chunk, nchunk, nh):
    @pl.when(pl.program_id(2) == 0)
    def _():
        st_scr[...] = jnp.zeros_like(st_scr)

    row = lax.broadcasted_iota(jnp.int32, (chunk, HEAD), 0)
    rr = lax.broadcasted_iota(jnp.int32, (chunk, chunk), 0)
    cc = lax.broadcasted_iota(jnp.int32, (chunk, chunk), 1)
    nlev = int(math.log2(chunk))
    pair = [((rr >> (lev + 1)) == (cc >> (lev + 1))) & ((rr & (1 << lev)) != 0) & ((cc & (1 << lev)) == 0)
            for lev in range(nlev)]

    def one_head(rows, hh):
        lanes = slice(hh * HEAD, (hh + 1) * HEAD)
        w = lf_ref[rows, lanes]
        qb = q_ref[rows, lanes]
        kb = k_ref[rows, lanes]
        q = qb.astype(F32)
        k = kb.astype(F32)
        v = v_ref[rows, lanes]
        attn = jnp.where(rr == cc, _dot_nt(qb, kb), 0.0)
        for lev in range(nlev):
            c = 1 << lev
            right = (row & c) != 0
            wl = _bcast_left_end(w, c, row)
            e = jnp.exp2(jnp.where(right, w, wl - w))
            p = _dot_nt((q * e).astype(BF16), (k * e).astype(BF16))
            attn = attn + jnp.where(pair[lev], p, 0.0)
            w = jnp.where(right, w + wl, w)
        bl = w[chunk - 1:chunk, :]
        qe = (q * jnp.exp2(w)).astype(BF16)
        ke = (k * jnp.exp2(bl - w)).astype(BF16)
        st = st_scr[hh]
        o = _dot(attn.astype(BF16), v) + _dot_nt(qe, st.astype(BF16))
        st_scr[hh] = st * jnp.exp2(bl) + _dot_tn(v, ke)
        o = o * lax.rsqrt(jnp.mean(o * o, axis=-1, keepdims=True) + RMS_EPS)
        o = o * ng_ref[:, lanes] * sg_ref[rows, lanes].astype(F32)
        o_ref[rows, lanes] = o.astype(BF16)

    def body(ci, carry):
        rows = pl.ds(pl.multiple_of(ci * chunk, chunk), chunk)
        for hh in range(nh):
            one_head(rows, hh)
        return carry

    lax.fori_loop(0, nchunk, body, 0, unroll=min(HG_UNROLL, nchunk))


def _hgrn(qf, lf, kk, vv, sg, norm_g, *, layer, batch, seq):
    t, hgw = qf.shape
    heads = hgw // HEAD
    nh = math.gcd(heads, HG_HEADS_PER_STEP)
    lb = min(512, seq)
    chunk = min(HG_CHUNK, lb)
    nsb = seq // lb
    spec = pl.BlockSpec((lb, nh * HEAD), lambda b, h, s: (b * nsb + s, h))
    return pl.pallas_call(
        functools.partial(_hgrn_kernel, chunk=chunk, nchunk=lb // chunk, nh=nh),
        out_shape=jax.ShapeDtypeStruct((t, hgw), BF16),
        grid=(batch, heads // nh, nsb),
        in_specs=[spec, spec, spec, spec, spec,
                  pl.BlockSpec((None, 1, nh * HEAD), lambda b, h, s: (layer, 0, h))],
        out_specs=spec,
        scratch_shapes=[pltpu.VMEM((nh, HEAD, HEAD), F32)],
        compiler_params=_cparams(("parallel", "parallel", "arbitrary")),
        name=f"hgrn_l{layer}",
    )(qf, lf, kk, vv, sg, norm_g.reshape(norm_g.shape[0], 1, hgw))


def _moba_kernel(q_ref, k_ref, v_ref, o_ref, kmh_scr, kml_scr, vt_scr, *, nblk, blk, nqb):
    tq = nqb * blk
    ntile = nblk // nqb
    nrow = kmh_scr.shape[0]
    seq = nblk * blk

    kmh_scr[...] = jnp.zeros_like(kmh_scr)
    kml_scr[...] = jnp.zeros_like(kml_scr)
    for j in range(nblk):
        km = jnp.sum(k_ref[j * blk:(j + 1) * blk, :].astype(F32), axis=0, keepdims=True) * (1.0 / blk)
        hi = km.astype(BF16)
        kmh_scr[j:j + 1, :] = hi
        kml_scr[j:j + 1, :] = (km - hi.astype(F32)).astype(BF16)
    ones_rows = (lax.broadcasted_iota(jnp.int32, (VT_PAD, tq), 0) == 0).astype(BF16)
    for g in range(ntile):
        vt_scr[g, :HEAD, :] = v_ref[g * tq:(g + 1) * tq, :].astype(F32).T.astype(BF16)
        vt_scr[g, HEAD:, :] = ones_rows

    scale = HEAD ** -0.5
    q_all = q_ref[...]
    gate = _dot_nt(kmh_scr[...], q_all) + _dot_nt(kml_scr[...], q_all)
    kb = lax.broadcasted_iota(jnp.int32, (nrow, seq), 0).astype(F32)
    qb = (lax.broadcasted_iota(jnp.int32, (1, seq), 1) // blk).astype(F32)
    g_ = jnp.where(kb < qb, gate, NEG)
    picks = []
    for _ in range(MB_TOPK):
        m = jnp.max(g_, axis=0, keepdims=True)
        idx = jnp.min(jnp.where(g_ == m, kb, float(nrow)), axis=0, keepdims=True)
        picks.append(jnp.where(idx < qb, idx, -1.0))
        g_ = jnp.where(kb == idx, -3e38, g_)

    c = scale * math.log2(math.e)
    tri = (lax.broadcasted_iota(jnp.int32, (blk, blk), 0) <= lax.broadcasted_iota(jnp.int32, (blk, blk), 1))
    ncb = math.gcd(nqb, MB_CHAIN_BLOCKS)
    nch = nqb // ncb
    cw = ncb * blk

    def scores(it, ch, g):
        own = g == it
        e0 = ch * ncb
        nkb = e0 + ncb if own else nqb
        q0 = (it * nqb + e0) * blk
        s = _dot_nt(k_ref[g * tq:g * tq + nkb * blk, :], q_ref[q0:q0 + cw, :])
        parts = []
        for d in range(nkb):
            sd = s[d * blk:(d + 1) * blk, :]
            j_f = float(g * nqb + d)
            cols = []
            for e in range(e0, e0 + ncb):
                sde = sd[:, (e - e0) * blk:(e - e0 + 1) * blk]
                lanes = slice((it * nqb + e) * blk, (it * nqb + e + 1) * blk)
                if own and d == e:
                    cols.append(jnp.where(tri, sde, NEG))
                elif own and d > e:
                    cols.append(jnp.full_like(sde, NEG))
                else:
                    sel = (picks[0][:, lanes] == j_f) | (picks[1][:, lanes] == j_f) | (picks[2][:, lanes] == j_f)
                    cols.append(sde + jnp.where(sel, 0.0, NEG))
            parts.append(cols[0] if ncb == 1 else jnp.concatenate(cols, axis=1))
        return parts[0] if nkb == 1 else jnp.concatenate(parts, axis=0)

    def update(g, s, carry):
        m_prev, acc = carry
        m_new = jnp.maximum(m_prev, jnp.max(s, axis=0, keepdims=True))
        a = jnp.exp2((m_prev - m_new) * c)
        p = jnp.exp2((s - m_new) * c)
        acc = a * acc + _dot(vt_scr[g, :, :s.shape[0]], p.astype(BF16))
        return m_new, acc

    steps = [(it, g) for it in range(ntile) for g in range(it + 1)]
    init = (jnp.full((1, cw), NEG, F32), jnp.zeros((HEAD + VT_PAD, cw), F32))
    cur = [scores(steps[0][0], ch, steps[0][1]) for ch in range(nch)]
    carry = [init] * nch
    for i, (it, g) in enumerate(steps):
        nxt = [scores(steps[i + 1][0], ch, steps[i + 1][1]) for ch in range(nch)] if i + 1 < len(steps) else None
        carry = [update(g, cur[ch], carry[ch]) for ch in range(nch)]
        if g == it:
            outs = [acc[:HEAD, :] / acc[HEAD:HEAD + 1, :] for _, acc in carry]
            out = outs[0] if nch == 1 else jnp.concatenate(outs, axis=1)
            o_ref[it * tq:(it + 1) * tq, :] = out.T.astype(BF16)
            carry = [init] * nch
        cur = nxt


def _moba(mq, mk, mv, *, layer, batch, seq):
    t, mbw = mq.shape
    heads = mbw // HEAD
    blk = MB_BLOCK
    nblk = seq // blk
    nqb = math.gcd(nblk, MB_TILE_BLOCKS)
    assert seq % blk == 0
    nrow = -(-nblk // 16) * 16
    spec = pl.BlockSpec((seq, HEAD), lambda b, h: (b, h))
    return pl.pallas_call(
        functools.partial(_moba_kernel, nblk=nblk, blk=blk, nqb=nqb),
        out_shape=jax.ShapeDtypeStruct((t, mbw), BF16),
        grid=(batch, heads),
        in_specs=[spec, spec, spec],
        out_specs=spec,
        scratch_shapes=[pltpu.VMEM((nrow, HEAD), BF16), pltpu.VMEM((nrow, HEAD), BF16),
                        pltpu.VMEM((nblk // nqb, HEAD + VT_PAD, nqb * blk), BF16)],
        compiler_params=_cparams(("parallel", "parallel")),
        name=f"moba_l{layer}",
    )(mq, mk, mv)


def _merge_kernel(*refs, alpha, ng):
    x_ref, mod_ref, oa_ref, ob_ref = refs[:4]
    wga_refs, wgb_refs = refs[4:4 + ng], refs[4 + ng:4 + 2 * ng]
    wa_ref, wb_ref, wo_ref, lng_ref, lnb_ref, o_ref = refs[4 + 2 * ng:]
    shift, scale, gate = _mod_rows(mod_ref, 1)
    x = x_ref[...]
    h = (x * (1.0 + scale) + shift).astype(BF16)

    def gates(w_refs):
        parts = [jax.nn.sigmoid(_dot(h, w[...])) for w in w_refs]
        return parts[0] if ng == 1 else jnp.concatenate(parts, axis=1)

    merged = gates(wga_refs) * _dot(oa_ref[...], wa_ref[...])
    merged = merged + gates(wgb_refs) * _dot(ob_ref[...], wb_ref[...])
    y = _dot(merged.astype(BF16), wo_ref[...])
    z = alpha * x + (1.0 + gate) * y
    o_ref[...] = _layer_norm(z, lng_ref[...], lnb_ref[...])


def _merge(x2, mod_l, oa, ob, w_in, wa, wb, wo, lng, lnb, *, layer, seq, alpha):
    t, d = x2.shape
    hgw, mbw = oa.shape[1], ob.shape[1]
    tm = min(256, seq)
    spt = seq // tm
    one = pl.Buffered(1)
    gw = math.gcd(d, w_in.shape[1] - 2 * d)
    ng = d // gw
    g0 = (w_in.shape[1] - 2 * d) // gw
    gspecs = [pl.BlockSpec((d, gw), lambda m, col=g0 + i: (0, col), pipeline_mode=one) for i in range(2 * ng)]
    return pl.pallas_call(
        functools.partial(_merge_kernel, alpha=alpha, ng=ng),
        out_shape=jax.ShapeDtypeStruct((t, d), F32),
        grid=(t // tm,),
        in_specs=[pl.BlockSpec((tm, d), lambda m: (m, 0)),
                  pl.BlockSpec((1, 3 * N_SUB, d), lambda m: (m // spt, 0, 0)),
                  pl.BlockSpec((tm, hgw), lambda m: (m, 0)),
                  pl.BlockSpec((tm, mbw), lambda m: (m, 0))] + gspecs + [
                  pl.BlockSpec((hgw, d), lambda m: (0, 0), pipeline_mode=one),
                  pl.BlockSpec((mbw, d), lambda m: (0, 0), pipeline_mode=one),
                  pl.BlockSpec((d, d), lambda m: (0, 0), pipeline_mode=one),
                  pl.BlockSpec((1, d), lambda m: (0, 0)),
                  pl.BlockSpec((1, d), lambda m: (0, 0))],
        out_specs=pl.BlockSpec((tm, d), lambda m: (m, 0)),
        compiler_params=_cparams(("parallel",)),
        name=f"merge_l{layer}",
    )(x2, mod_l, oa, ob, *([w_in] * (2 * ng)), wa, wb, wo, lng, lnb)


def kernel(x, c, ada_w, ada_b, ln_g, ln_b, ffn_w_gate, ffn_w_up, ffn_w_down, w_in, hg_lb_logits,
           hg_norm_g, w_branch_a, w_branch_b, w_out):
    batch, seq, d = x.shape
    depth = ada_w.shape[0]
    hgw = hg_norm_g.shape[1]
    mbw = w_branch_b.shape[1]
    assert w_in.shape[2] == 4 * hgw + 3 * mbw + 2 * d and hgw == mbw and d % hgw == 0
    alpha = (2 * depth) ** 0.25

    c_pad = jnp.pad(c, ((0, 8 - batch % 8 if batch % 8 else 0), (0, 0)))
    mod = _adaln(c_pad, ada_w, ada_b)[:, :batch].reshape(depth, batch, 3 * N_SUB, d)

    def ffn_w(l, j):
        return [(ffn_w_gate, (l, j)), (ffn_w_up, (l, j)), (ffn_w_down, (l, j))]

    def branch_w(l):
        return [(w_branch_a, (l,)), (w_branch_b, (l,)), (w_out, (l,))]

    ffn0 = [w[0, 0].astype(BF16) for w in (ffn_w_gate, ffn_w_up, ffn_w_down)]
    win = w_in[0].astype(BF16)
    wa, wb, wo = [w[0].astype(BF16) for w in (w_branch_a, w_branch_b, w_out)]

    x2 = x.reshape(batch * seq, d)
    for l in range(depth):
        more = l + 1 < depth
        mod_l = mod[l]
        lng = [ln_g[l, j].reshape(1, d) for j in range(N_SUB)]
        lnb = [ln_b[l, j].reshape(1, d) for j in range(N_SUB)]
        x2, _ = _ffn(x2, mod_l, *ffn0, lng[0], lnb[0], [], layer=l, which=0, sub=0, seq=seq, alpha=alpha)
        (qf, lf, kk, vv, sg), cast_h = _proj_hg(x2, mod_l, hg_lb_logits, win,
                                               ffn_w(l, 1) + (branch_w(l + 1) if more else []),
                                               layer=l, seq=seq, hgw=hgw)
        (mq, mk, mv), cast_m = _proj_mb(x2, mod_l, win, ffn_w(l + 1, 0) if more else [],
                                        layer=l, seq=seq, mbw=mbw, col0=4 * hgw // mbw)
        oa = _hgrn(qf, lf, kk, vv, sg, hg_norm_g, layer=l, batch=batch, seq=seq)
        ob = _moba(mq, mk, mv, layer=l, batch=batch, seq=seq)
        x2 = _merge(x2, mod_l, oa, ob, win, wa, wb, wo, lng[1], lnb[1], layer=l, seq=seq, alpha=alpha)
        x2, cast_f = _ffn(x2, mod_l, *cast_h[:3], lng[2], lnb[2], [(w_in, (l + 1,))] if more else [],
                          layer=l, which=1, sub=2, seq=seq, alpha=alpha)
        if more:
            ffn0, (wa, wb, wo), (win,) = cast_m, cast_h[3:], cast_f
    return x2.reshape(batch, seq, d)
```

```python
import functools
import math

import jax
import jax.numpy as jnp
from jax import lax
from jax.experimental import pallas as pl
from jax.experimental.pallas import tpu as pltpu

F32 = jnp.float32
BF16 = jnp.bfloat16

HEAD = 128
MB_BLOCK = 256
MB_TOPK = 3
MB_TILE_BLOCKS = 4
MB_CHAIN_BLOCKS = 2
VT_PAD = 16
HG_CHUNK = 64
HG_UNROLL = 8
HG_HEADS_PER_STEP = 2
F_MIN = 1e-30
MACARON_W = 0.5
N_SUB = 3
LN_EPS = 1e-5
RMS_EPS = 1e-6
NEG = -1e30
LANES = 128
VMEM_LIMIT = 56 * 1024 * 1024
FFN_VMEM_LIMIT = 60 * 1024 * 1024
FFN_ROWS = 1024
FFN_DOWN_COLS = 512


def _cparams(sem):
    return pltpu.CompilerParams(dimension_semantics=sem, vmem_limit_bytes=VMEM_LIMIT)


def _dot(a, b):
    return jnp.dot(a, b, preferred_element_type=F32)


def _dot_nt(a, b):
    return lax.dot_general(a, b, (((1,), (1,)), ((), ())), preferred_element_type=F32)


def _dot_tn(a, b):
    return lax.dot_general(a, b, (((0,), (0,)), ((), ())), preferred_element_type=F32)


def _silu(x):
    return x * jax.nn.sigmoid(x)


def _layer_norm(z, g, b):
    mu = jnp.mean(z, axis=-1, keepdims=True)
    zc = z - mu
    var = jnp.mean(zc * zc, axis=-1, keepdims=True)
    return zc * lax.rsqrt(var + LN_EPS) * g + b


def _mod_rows(mod_ref, sub):
    return (mod_ref[0, 3 * sub:3 * sub + 1, :], mod_ref[0, 3 * sub + 1:3 * sub + 2, :],
            mod_ref[0, 3 * sub + 2:3 * sub + 3, :])


def _adaln_kernel(c_ref, w_ref, b_ref, o_ref):
    cond = _silu(c_ref[...])
    o_ref[...] = _dot(cond, w_ref[...]) + b_ref[...]


def _adaln(c_pad, ada_w, ada_b):
    depth, d, n = ada_w.shape
    tn = min(1024, d)
    return pl.pallas_call(
        _adaln_kernel,
        out_shape=jax.ShapeDtypeStruct((depth, c_pad.shape[0], n), F32),
        grid=(depth, n // tn),
        in_specs=[pl.BlockSpec(c_pad.shape, lambda l, j: (0, 0)),
                  pl.BlockSpec((None, d, tn), lambda l, j: (l, 0, j)),
                  pl.BlockSpec((None, 1, tn), lambda l, j: (l, 0, j))],
        out_specs=pl.BlockSpec((None, c_pad.shape[0], tn), lambda l, j: (l, 0, j)),
        compiler_params=_cparams(("arbitrary", "arbitrary")),
        name="adaln",
    )(c_pad, ada_w, ada_b.reshape(depth, 1, n))


def _side_rows(r, nsteps):
    rb = 16 * -(-r // (16 * nsteps))
    while rb < r and r % rb:
        rb += 16
    return min(rb, r)


def _side_specs(side, nsteps, step):
    arrays, in_specs, out_specs, out_shapes = [], [], [], []
    for arr, lead in side:
        r, c = arr.shape[-2:]
        rb = _side_rows(r, nsteps)
        last = r // rb - 1
        arrays.append(arr)
        in_specs.append(pl.BlockSpec(
            (None,) * len(lead) + (rb, c),
            lambda *ids, lead=lead, last=last: lead + (jnp.minimum(step(*ids), last), 0)))
        out_specs.append(pl.BlockSpec((rb, c), lambda *ids, last=last: (jnp.minimum(step(*ids), last), 0)))
        out_shapes.append(jax.ShapeDtypeStruct((r, c), BF16))
    return arrays, in_specs, out_specs, out_shapes


def _cast_side(side_in, side_out):
    for src, dst in zip(side_in, side_out):
        dst[...] = src[...].astype(BF16)


def _ffn_kernel(x_hbm, mod_ref, wg_ref, wu_ref, wd_ref, lng_ref, lnb_ref, o_ref, h_scr, acc_scr, x_buf, x_sem,
                *, sub, alpha, nf, nm, tm):
    m = pl.program_id(0)
    f = pl.program_id(1)
    shift, scale, gate = _mod_rows(mod_ref, sub)

    def x_copy(tile):
        return pltpu.make_async_copy(x_hbm.at[pl.ds(tile * tm, tm), :], x_buf, x_sem)

    @pl.when((m == 0) & (f == 0))
    def _():
        x_copy(0).start()

    @pl.when(f == 0)
    def _():
        x_copy(m).wait()
        x = x_buf[...]
        h_scr[...] = (x * (1.0 + scale) + shift).astype(BF16)
        acc_scr[...] = alpha * x

    prefetch_step = min(1, nf - 1)

    @pl.when((f == prefetch_step) & (m + 1 < nm))
    def _():
        x_copy(m + 1).start()

    h = h_scr[...]
    g = _dot(h, wg_ref[...])
    u = _dot(h, wu_ref[...])
    a = (_silu(g) * u).astype(BF16)
    cvec = MACARON_W * (1.0 + gate)
    d = acc_scr.shape[1]
    cw = FFN_DOWN_COLS if d % FFN_DOWN_COLS == 0 else d
    for n in range(d // cw):
        cols = slice(n * cw, (n + 1) * cw)
        acc_scr[:, cols] += _dot(a, wd_ref[:, cols]) * cvec[:, cols]

    @pl.when(f == nf - 1)
    def _():
        o_ref[...] = _layer_norm(acc_scr[...], lng_ref[...], lnb_ref[...])


def _ffn(x2, mod_l, wg, wu, wd, lng, lnb, *, layer, which, sub, seq, alpha):
    t, d = x2.shape
    dff = wg.shape[-1]
    tm = min(FFN_ROWS, seq)
    tf = 512 if dff % 512 == 0 else dff
    nf = dff // tf
    nm = t // tm
    spt = seq // tm
    return pl.pallas_call(
        functools.partial(_ffn_kernel, sub=sub, alpha=alpha, nf=nf, nm=nm, tm=tm),
        out_shape=jax.ShapeDtypeStruct((t, d), F32),
        grid=(nm, nf),
        in_specs=[pl.BlockSpec(memory_space=pl.ANY),
                  pl.BlockSpec((1, 3 * N_SUB, d), lambda m, f: (m // spt, 0, 0)),
                  pl.BlockSpec((d, tf), lambda m, f: (0, f)),
                  pl.BlockSpec((d, tf), lambda m, f: (0, f)),
                  pl.BlockSpec((tf, d), lambda m, f: (f, 0)),
                  pl.BlockSpec((1, d), lambda m, f: (0, 0)),
                  pl.BlockSpec((1, d), lambda m, f: (0, 0))],
        out_specs=pl.BlockSpec((tm, d), lambda m, f: (m, 0)),
        scratch_shapes=[pltpu.VMEM((tm, d), BF16), pltpu.VMEM((tm, d), F32), pltpu.VMEM((tm, d), F32),
                        pltpu.SemaphoreType.DMA(())],
        compiler_params=pltpu.CompilerParams(dimension_semantics=("arbitrary", "arbitrary"),
                                             vmem_limit_bytes=FFN_VMEM_LIMIT),
        name=f"ffn_l{layer}_{which}",
    )(x2, mod_l, wg, wu, wd, lng, lnb)


def _proj_hg_kernel(*refs, layer, nside):
    x_ref, mod_ref, lbl_ref, wq_ref, wf_ref, wi_ref, wg_ref = refs[:7]
    q_ref, lf_ref, k_ref, v_ref, sg_ref = refs[7 + nside:12 + nside]
    _cast_side(refs[7:7 + nside], refs[12 + nside:])
    shift, scale, _ = _mod_rows(mod_ref, 1)
    h = (x_ref[...] * (1.0 + scale) + shift).astype(BF16)
    logits = lbl_ref[...]
    e = jnp.exp(logits - jnp.max(logits, axis=0, keepdims=True))
    p = e / jnp.sum(e, axis=0, keepdims=True)
    lb = jnp.zeros_like(p[0:1, :])
    for i in range(1, layer + 1):
        lb = lb + p[i:i + 1, :]
    q_ref[...] = _silu(_dot(h, wq_ref[...])).astype(BF16)
    z = _dot(h, wf_ref[...])
    fgate = lb + (1.0 - lb) * jax.nn.sigmoid(z)
    lf_ref[...] = jnp.log2(jnp.maximum(fgate, F_MIN))
    k_ref[...] = ((1.0 - lb) * jax.nn.sigmoid(-z)).astype(BF16)
    v_ref[...] = _dot(h, wi_ref[...]).astype(BF16)
    sg_ref[...] = _silu(_dot(h, wg_ref[...])).astype(BF16)


def _proj_hg(x2, mod_l, lb_logits, w_in, side, *, layer, seq, hgw):
    t, d = x2.shape
    tm = min(512, seq)
    spt = seq // tm
    depth = lb_logits.shape[0]
    side_arrays, side_in, side_out, side_shapes = _side_specs(side, t // tm, lambda m: m)

    def wspec(col):
        return pl.BlockSpec((d, hgw), lambda m, col=col: (0, col), pipeline_mode=pl.Buffered(1))

    out = jax.ShapeDtypeStruct((t, hgw), BF16)
    ospec = pl.BlockSpec((tm, hgw), lambda m: (m, 0))
    outs = pl.pallas_call(
        functools.partial(_proj_hg_kernel, layer=layer, nside=len(side)),
        out_shape=[out, jax.ShapeDtypeStruct((t, hgw), F32), out, out, out] + side_shapes,
        grid=(t // tm,),
        in_specs=[pl.BlockSpec((tm, d), lambda m: (m, 0)),
                  pl.BlockSpec((1, 3 * N_SUB, d), lambda m: (m // spt, 0, 0)),
                  pl.BlockSpec((depth, hgw), lambda m: (0, 0)),
                  wspec(0), wspec(1), wspec(2), wspec(3)] + side_in,
        out_specs=[ospec, ospec, ospec, ospec, ospec] + side_out,
        compiler_params=_cparams(("arbitrary",)),
        name=f"proj_hg_l{layer}",
    )(x2, mod_l, lb_logits, w_in, w_in, w_in, w_in, *side_arrays)
    return outs[:5], outs[5:]


def _proj_mb_kernel(*refs, nside):
    x_ref, mod_ref, wq_ref, wk_ref, wv_ref = refs[:5]
    q_ref, k_ref, v_ref = refs[5 + nside:8 + nside]
    _cast_side(refs[5:5 + nside], refs[8 + nside:])
    shift, scale, _ = _mod_rows(mod_ref, 1)
    h = (x_ref[...] * (1.0 + scale) + shift).astype(BF16)
    q_ref[...] = _dot(h, wq_ref[...]).astype(BF16)
    k_ref[...] = _dot(h, wk_ref[...]).astype(BF16)
    v_ref[...] = _dot(h, wv_ref[...]).astype(BF16)


def _proj_mb(x2, mod_l, w_in, side, *, layer, seq, mbw, col0):
    t, d = x2.shape
    tm = min(512, seq)
    spt = seq // tm
    side_arrays, side_in, side_out, side_shapes = _side_specs(side, t // tm, lambda m: m)

    def wspec(col):
        return pl.BlockSpec((d, mbw), lambda m, col=col: (0, col), pipeline_mode=pl.Buffered(1))

    out = jax.ShapeDtypeStruct((t, mbw), BF16)
    ospec = pl.BlockSpec((tm, mbw), lambda m: (m, 0))
    outs = pl.pallas_call(
        functools.partial(_proj_mb_kernel, nside=len(side)),
        out_shape=[out, out, out] + side_shapes,
        grid=(t // tm,),
        in_specs=[pl.BlockSpec((tm, d), lambda m: (m, 0)),
                  pl.BlockSpec((1, 3 * N_SUB, d), lambda m: (m // spt, 0, 0)),
                  wspec(col0), wspec(col0 + 1), wspec(col0 + 2)] + side_in,
        out_specs=[ospec, ospec, ospec] + side_out,
        compiler_params=_cparams(("arbitrary",)),
        name=f"proj_mb_l{layer}",
    )(x2, mod_l, w_in, w_in, w_in, *side_arrays)
    return outs[:3], outs[3:]


def _bcast_left_end(w, c, row):
    n = w.shape[0]
    if c >= 8:
        parts = []
        for g in range(n // (2 * c)):
            r = g * 2 * c + c - 1
            parts.append(jnp.broadcast_to(w[r:r + 1, :], (2 * c, w.shape[1])))
        return parts[0] if len(parts) == 1 else jnp.concatenate(parts, axis=0)
    up = lambda x, s: pltpu.roll(x, n - s, 0)
    down = lambda x, s: pltpu.roll(x, s, 0)
    if c == 1:
        return jnp.where((row & 1) == 1, down(w, 1), w)
    if c == 2:
        z = jnp.where((row & 3) == 0, up(w, 1), w)
        return jnp.where((row & 3) >= 2, down(z, 2), z)
    z = jnp.where((row & 7) == 2, up(w, 1), w)
    z = jnp.where((row & 7) < 2, up(z, 2), z)
    return jnp.where((row & 7) >= 4, down(z, 4), z)


def _hgrn_kernel(*refs, chunk, nchunk, nh, nside):
    q_ref, lf_ref, k_ref, v_ref, sg_ref, ng_ref = refs[:6]
    o_ref = refs[6 + nside]
    st_scr = refs[-1]
    _cast_side(refs[6:6 + nside], refs[7 + nside:-1])

    @pl.when(pl.program_id(2) == 0)
    def _():
        st_scr[...] = jnp.zeros_like(st_scr)

    row = lax.broadcasted_iota(jnp.int32, (chunk, HEAD), 0)
    rr = lax.broadcasted_iota(jnp.int32, (chunk, chunk), 0)
    cc = lax.broadcasted_iota(jnp.int32, (chunk, chunk), 1)
    nlev = int(math.log2(chunk))
    pair = [((rr >> (lev + 1)) == (cc >> (lev + 1))) & ((rr & (1 << lev)) != 0) & ((cc & (1 << lev)) == 0)
            for lev in range(nlev)]

    def one_head(rows, hh):
        lanes = slice(hh * HEAD, (hh + 1) * HEAD)
        w = lf_ref[rows, lanes]
        qb = q_ref[rows, lanes]
        kb = k_ref[rows, lanes]
        q = qb.astype(F32)
        k = kb.astype(F32)
        v = v_ref[rows, lanes]
        attn = jnp.where(rr == cc, _dot_nt(qb, kb), 0.0)
        for lev in range(nlev):
            c = 1 << lev
            right = (row & c) != 0
            wl = _bcast_left_end(w, c, row)
            e = jnp.exp2(jnp.where(right, w, wl - w))
            p = _dot_nt((q * e).astype(BF16), (k * e).astype(BF16))
            attn = attn + jnp.where(pair[lev], p, 0.0)
            w = jnp.where(right, w + wl, w)
        bl = w[chunk - 1:chunk, :]
        qe = (q * jnp.exp2(w)).astype(BF16)
        ke = (k * jnp.exp2(bl - w)).astype(BF16)
        st = st_scr[hh]
        o = _dot(attn.astype(BF16), v) + _dot_nt(qe, st.astype(BF16))
        st_scr[hh] = st * jnp.exp2(bl) + _dot_tn(v, ke)
        o = o * lax.rsqrt(jnp.mean(o * o, axis=-1, keepdims=True) + RMS_EPS)
        o = o * ng_ref[:, lanes] * sg_ref[rows, lanes].astype(F32)
        o_ref[rows, lanes] = o.astype(BF16)

    def body(ci, carry):
        rows = pl.ds(pl.multiple_of(ci * chunk, chunk), chunk)
        for hh in range(nh):
            one_head(rows, hh)
        return carry

    lax.fori_loop(0, nchunk, body, 0, unroll=min(HG_UNROLL, nchunk))


def _hgrn(qf, lf, kk, vv, sg, norm_g, side, *, layer, batch, seq):
    t, hgw = qf.shape
    heads = hgw // HEAD
    nh = math.gcd(heads, HG_HEADS_PER_STEP)
    lb = min(512, seq)
    chunk = min(HG_CHUNK, lb)
    nsb = seq // lb
    nhp = heads // nh
    side_arrays, side_in, side_out, side_shapes = _side_specs(
        side, batch * nhp * nsb, lambda b, h, s: (b * nhp + h) * nsb + s)
    spec = pl.BlockSpec((lb, nh * HEAD), lambda b, h, s: (b * nsb + s, h))
    outs = pl.pallas_call(
        functools.partial(_hgrn_kernel, chunk=chunk, nchunk=lb // chunk, nh=nh, nside=len(side)),
        out_shape=[jax.ShapeDtypeStruct((t, hgw), BF16)] + side_shapes,
        grid=(batch, nhp, nsb),
        in_specs=[spec, spec, spec, spec, spec,
                  pl.BlockSpec((None, 1, nh * HEAD), lambda b, h, s: (layer, 0, h))] + side_in,
        out_specs=[spec] + side_out,
        scratch_shapes=[pltpu.VMEM((nh, HEAD, HEAD), F32)],
        compiler_params=_cparams(("arbitrary", "arbitrary", "arbitrary")),
        name=f"hgrn_l{layer}",
    )(qf, lf, kk, vv, sg, norm_g.reshape(norm_g.shape[0], 1, hgw), *side_arrays)
    return outs[0], outs[1:]


def _moba_kernel(q_ref, k_ref, v_ref, o_ref, kmh_scr, kml_scr, vt_scr, *, nblk, blk, nqb):
    tq = nqb * blk
    ntile = nblk // nqb
    nrow = kmh_scr.shape[0]
    seq = nblk * blk

    kmh_scr[...] = jnp.zeros_like(kmh_scr)
    kml_scr[...] = jnp.zeros_like(kml_scr)
    for j in range(nblk):
        km = jnp.sum(k_ref[j * blk:(j + 1) * blk, :].astype(F32), axis=0, keepdims=True) * (1.0 / blk)
        hi = km.astype(BF16)
        kmh_scr[j:j + 1, :] = hi
        kml_scr[j:j + 1, :] = (km - hi.astype(F32)).astype(BF16)
    ones_rows = (lax.broadcasted_iota(jnp.int32, (VT_PAD, tq), 0) == 0).astype(BF16)
    for g in range(ntile):
        vt_scr[g, :HEAD, :] = v_ref[g * tq:(g + 1) * tq, :].astype(F32).T.astype(BF16)
        vt_scr[g, HEAD:, :] = ones_rows

    scale = HEAD ** -0.5
    q_all = q_ref[...]
    gate = _dot_nt(kmh_scr[...], q_all) + _dot_nt(kml_scr[...], q_all)
    kb = lax.broadcasted_iota(jnp.int32, (nrow, seq), 0).astype(F32)
    qb = (lax.broadcasted_iota(jnp.int32, (1, seq), 1) // blk).astype(F32)
    g_ = jnp.where(kb < qb, gate, NEG)
    picks = []
    for _ in range(MB_TOPK):
        m = jnp.max(g_, axis=0, keepdims=True)
        idx = jnp.min(jnp.where(g_ == m, kb, float(nrow)), axis=0, keepdims=True)
        picks.append(jnp.where(idx < qb, idx, -1.0))
        g_ = jnp.where(kb == idx, -3e38, g_)

    c = scale * math.log2(math.e)
    tri = (lax.broadcasted_iota(jnp.int32, (blk, blk), 0) <= lax.broadcasted_iota(jnp.int32, (blk, blk), 1))
    ncb = math.gcd(nqb, MB_CHAIN_BLOCKS)
    nch = nqb // ncb
    cw = ncb * blk

    def scores(it, ch, g):
        own = g == it
        e0 = ch * ncb
        nkb = e0 + ncb if own else nqb
        q0 = (it * nqb + e0) * blk
        s = _dot_nt(k_ref[g * tq:g * tq + nkb * blk, :], q_ref[q0:q0 + cw, :])
        parts = []
        for d in range(nkb):
            sd = s[d * blk:(d + 1) * blk, :]
            j_f = float(g * nqb + d)
            cols = []
            for e in range(e0, e0 + ncb):
                sde = sd[:, (e - e0) * blk:(e - e0 + 1) * blk]
                lanes = slice((it * nqb + e) * blk, (it * nqb + e + 1) * blk)
                if own and d == e:
                    cols.append(jnp.where(tri, sde, NEG))
                elif own and d > e:
                    cols.append(jnp.full_like(sde, NEG))
                else:
                    sel = (picks[0][:, lanes] == j_f) | (picks[1][:, lanes] == j_f) | (picks[2][:, lanes] == j_f)
                    cols.append(sde + jnp.where(sel, 0.0, NEG))
            parts.append(cols[0] if ncb == 1 else jnp.concatenate(cols, axis=1))
        return parts[0] if nkb == 1 else jnp.concatenate(parts, axis=0)

    def update(g, s, carry):
        m_prev, acc = carry
        m_new = jnp.maximum(m_prev, jnp.max(s, axis=0, keepdims=True))
        a = jnp.exp2((m_prev - m_new) * c)
        p = jnp.exp2((s - m_new) * c)
        acc = a * acc + _dot(vt_scr[g, :, :s.shape[0]], p.astype(BF16))
        return m_new, acc

    steps = [(it, g) for it in range(ntile) for g in range(it + 1)]
    init = (jnp.full((1, cw), NEG, F32), jnp.zeros((HEAD + VT_PAD, cw), F32))
    cur = [scores(steps[0][0], ch, steps[0][1]) for ch in range(nch)]
    carry = [init] * nch
    for i, (it, g) in enumerate(steps):
        nxt = [scores(steps[i + 1][0], ch, steps[i + 1][1]) for ch in range(nch)] if i + 1 < len(steps) else None
        carry = [update(g, cur[ch], carry[ch]) for ch in range(nch)]
        if g == it:
            outs = [acc[:HEAD, :] / acc[HEAD:HEAD + 1, :] for _, acc in carry]
            out = outs[0] if nch == 1 else jnp.concatenate(outs, axis=1)
            o_ref[it * tq:(it + 1) * tq, :] = out.T.astype(BF16)
            carry = [init] * nch
        cur = nxt


def _moba(mq, mk, mv, *, layer, batch, seq):
    t, mbw = mq.shape
    heads = mbw // HEAD
    blk = MB_BLOCK
    nblk = seq // blk
    nqb = math.gcd(nblk, MB_TILE_BLOCKS)
    assert seq % blk == 0
    nrow = -(-nblk // 16) * 16
    spec = pl.BlockSpec((seq, HEAD), lambda b, h: (b, h))
    return pl.pallas_call(
        functools.partial(_moba_kernel, nblk=nblk, blk=blk, nqb=nqb),
        out_shape=jax.ShapeDtypeStruct((t, mbw), BF16),
        grid=(batch, heads),
        in_specs=[spec, spec, spec],
        out_specs=spec,
        scratch_shapes=[pltpu.VMEM((nrow, HEAD), BF16), pltpu.VMEM((nrow, HEAD), BF16),
                        pltpu.VMEM((nblk // nqb, HEAD + VT_PAD, nqb * blk), BF16)],
        compiler_params=_cparams(("parallel", "parallel")),
        name=f"moba_l{layer}",
    )(mq, mk, mv)


def _merge_kernel(*refs, alpha, ng):
    x_ref, mod_ref, oa_ref, ob_ref = refs[:4]
    wga_refs, wgb_refs = refs[4:4 + ng], refs[4 + ng:4 + 2 * ng]
    wa_ref, wb_ref, wo_ref, lng_ref, lnb_ref, o_ref = refs[4 + 2 * ng:]
    shift, scale, gate = _mod_rows(mod_ref, 1)
    x = x_ref[...]
    h = (x * (1.0 + scale) + shift).astype(BF16)

    def gates(w_refs):
        parts = [jax.nn.sigmoid(_dot(h, w[...])) for w in w_refs]
        return parts[0] if ng == 1 else jnp.concatenate(parts, axis=1)

    merged = gates(wga_refs) * _dot(oa_ref[...], wa_ref[...])
    merged = merged + gates(wgb_refs) * _dot(ob_ref[...], wb_ref[...])
    y = _dot(merged.astype(BF16), wo_ref[...])
    z = alpha * x + (1.0 + gate) * y
    o_ref[...] = _layer_norm(z, lng_ref[...], lnb_ref[...])


def _merge(x2, mod_l, oa, ob, w_in, wa, wb, wo, lng, lnb, *, layer, seq, alpha):
    t, d = x2.shape
    hgw, mbw = oa.shape[1], ob.shape[1]
    tm = min(256, seq)
    spt = seq // tm
    one = pl.Buffered(1)
    gw = math.gcd(d, w_in.shape[1] - 2 * d)
    ng = d // gw
    g0 = (w_in.shape[1] - 2 * d) // gw
    gspecs = [pl.BlockSpec((d, gw), lambda m, col=g0 + i: (0, col), pipeline_mode=one) for i in range(2 * ng)]
    return pl.pallas_call(
        functools.partial(_merge_kernel, alpha=alpha, ng=ng),
        out_shape=jax.ShapeDtypeStruct((t, d), F32),
        grid=(t // tm,),
        in_specs=[pl.BlockSpec((tm, d), lambda m: (m, 0)),
                  pl.BlockSpec((1, 3 * N_SUB, d), lambda m: (m // spt, 0, 0)),
                  pl.BlockSpec((tm, hgw), lambda m: (m, 0)),
                  pl.BlockSpec((tm, mbw), lambda m: (m, 0))] + gspecs + [
                  pl.BlockSpec((hgw, d), lambda m: (0, 0), pipeline_mode=one),
                  pl.BlockSpec((mbw, d), lambda m: (0, 0), pipeline_mode=one),
                  pl.BlockSpec((d, d), lambda m: (0, 0), pipeline_mode=one),
                  pl.BlockSpec((1, d), lambda m: (0, 0)),
                  pl.BlockSpec((1, d), lambda m: (0, 0))],
        out_specs=pl.BlockSpec((tm, d), lambda m: (m, 0)),
        compiler_params=_cparams(("parallel",)),
        name=f"merge_l{layer}",
    )(x2, mod_l, oa, ob, *([w_in] * (2 * ng)), wa, wb, wo, lng, lnb)


def kernel(x, c, ada_w, ada_b, ln_g, ln_b, ffn_w_gate, ffn_w_up, ffn_w_down, w_in, hg_lb_logits,
           hg_norm_g, w_branch_a, w_branch_b, w_out):
    batch, seq, d = x.shape
    depth = ada_w.shape[0]
    hgw = hg_norm_g.shape[1]
    mbw = w_branch_b.shape[1]
    assert w_in.shape[2] == 4 * hgw + 3 * mbw + 2 * d and hgw == mbw and d % hgw == 0
    alpha = (2 * depth) ** 0.25

    c_pad = jnp.pad(c, ((0, 8 - batch % 8 if batch % 8 else 0), (0, 0)))
    mod = _adaln(c_pad, ada_w, ada_b)[:, :batch].reshape(depth, batch, 3 * N_SUB, d)

    def ffn_w(l, j):
        return [(ffn_w_gate, (l, j)), (ffn_w_up, (l, j)), (ffn_w_down, (l, j))]

    def branch_w(l):
        return [(w_branch_a, (l,)), (w_branch_b, (l,)), (w_out, (l,))]

    ffn0 = [w[0, 0].astype(BF16) for w in (ffn_w_gate, ffn_w_up, ffn_w_down)]
    win = w_in[0].astype(BF16)
    wa, wb, wo = [w[0].astype(BF16) for w in (w_branch_a, w_branch_b, w_out)]

    x2 = x.reshape(batch * seq, d)
    for l in range(depth):
        more = l + 1 < depth
        mod_l = mod[l]
        lng = [ln_g[l, j].reshape(1, d) for j in range(N_SUB)]
        lnb = [ln_b[l, j].reshape(1, d) for j in range(N_SUB)]
        x2 = _ffn(x2, mod_l, *ffn0, lng[0], lnb[0], layer=l, which=0, sub=0, seq=seq, alpha=alpha)
        (qf, lf, kk, vv, sg), cast_h = _proj_hg(x2, mod_l, hg_lb_logits, win,
                                               ffn_w(l, 1) + (branch_w(l + 1) if more else []),
                                               layer=l, seq=seq, hgw=hgw)
        (mq, mk, mv), cast_m = _proj_mb(x2, mod_l, win, ffn_w(l + 1, 0) if more else [],
                                        layer=l, seq=seq, mbw=mbw, col0=4 * hgw // mbw)
        oa, cast_g = _hgrn(qf, lf, kk, vv, sg, hg_norm_g, [(w_in, (l + 1,))] if more else [],
                           layer=l, batch=batch, seq=seq)
        ob = _moba(mq, mk, mv, layer=l, batch=batch, seq=seq)
        x2 = _merge(x2, mod_l, oa, ob, win, wa, wb, wo, lng[1], lnb[1], layer=l, seq=seq, alpha=alpha)
        x2 = _ffn(x2, mod_l, *cast_h[:3], lng[2], lnb[2], layer=l, which=1, sub=2, seq=seq, alpha=alpha)
        if more:
            ffn0, (wa, wb, wo), (win,) = cast_m, cast_h[3:], cast_g
    return x2.reshape(batch, seq, d)
```

```python
import functools
import math

import jax
import jax.numpy as jnp
from jax import lax
from jax.experimental import pallas as pl
from jax.experimental.pallas import tpu as pltpu

F32 = jnp.float32
BF16 = jnp.bfloat16

HEAD = 128
MB_BLOCK = 256
MB_TOPK = 3
MB_TILE_BLOCKS = 4
MB_CHAIN_BLOCKS = 2
VT_PAD = 16
HG_CHUNK = 128
HG_BLOCK = 1024
HG_UNROLL = 8
HG_HEADS_PER_STEP = 2
F_MIN = 1e-30
MACARON_W = 0.5
N_SUB = 3
LN_EPS = 1e-5
RMS_EPS = 1e-6
NEG = -1e30
LANES = 128
VMEM_LIMIT = 56 * 1024 * 1024
FFN_VMEM_LIMIT = 60 * 1024 * 1024
FFN_ROWS = 1024
FFN_DOWN_COLS = 512


def _cparams(sem):
    return pltpu.CompilerParams(dimension_semantics=sem, vmem_limit_bytes=VMEM_LIMIT)


def _dot(a, b):
    return jnp.dot(a, b, preferred_element_type=F32)


def _dot_nt(a, b):
    return lax.dot_general(a, b, (((1,), (1,)), ((), ())), preferred_element_type=F32)


def _dot_tn(a, b):
    return lax.dot_general(a, b, (((0,), (0,)), ((), ())), preferred_element_type=F32)


def _silu(x):
    return x * jax.nn.sigmoid(x)


def _layer_norm(z, g, b):
    mu = jnp.mean(z, axis=-1, keepdims=True)
    zc = z - mu
    var = jnp.mean(zc * zc, axis=-1, keepdims=True)
    return zc * lax.rsqrt(var + LN_EPS) * g + b


def _mod_rows(mod_ref, sub):
    return (mod_ref[0, 3 * sub:3 * sub + 1, :], mod_ref[0, 3 * sub + 1:3 * sub + 2, :],
            mod_ref[0, 3 * sub + 2:3 * sub + 3, :])


def _adaln_kernel(c_ref, w_ref, b_ref, o_ref):
    cond = _silu(c_ref[...])
    o_ref[...] = _dot(cond, w_ref[...]) + b_ref[...]


def _adaln_specs(c_pad, ada_w, layer, step, max_steps=None):
    _, d, n = ada_w.shape
    tn = min(1024, d)
    while max_steps is not None and (n % tn or n // tn > max_steps):
        tn += LANES
    last = n // tn - 1
    col = lambda *ids: jnp.minimum(step(*ids), last)
    in_specs = [pl.BlockSpec(c_pad.shape, lambda *ids: (0, 0)),
                pl.BlockSpec((None, d, tn), lambda *ids: (layer, 0, col(*ids))),
                pl.BlockSpec((None, 1, tn), lambda *ids: (layer, 0, col(*ids)))]
    out_spec = pl.BlockSpec((c_pad.shape[0], tn), lambda *ids: (0, col(*ids)))
    return in_specs, out_spec, jax.ShapeDtypeStruct((c_pad.shape[0], n), F32), n // tn


def _adaln(c_pad, ada_w, ada_b, layer):
    depth, _, n = ada_w.shape
    in_specs, out_spec, out_shape, nsteps = _adaln_specs(c_pad, ada_w, layer, lambda j: j)
    return pl.pallas_call(
        _adaln_kernel,
        out_shape=out_shape,
        grid=(nsteps,),
        in_specs=in_specs,
        out_specs=out_spec,
        compiler_params=_cparams(("arbitrary",)),
        name="adaln",
    )(c_pad, ada_w, ada_b.reshape(depth, 1, n))


def _side_rows(r, nsteps):
    rb = 16 * -(-r // (16 * nsteps))
    while rb < r and r % rb:
        rb += 16
    return min(rb, r)


def _side_specs(side, nsteps, step):
    arrays, in_specs, out_specs, out_shapes = [], [], [], []
    for arr, lead in side:
        r, c = arr.shape[-2:]
        rb = _side_rows(r, nsteps)
        last = r // rb - 1
        arrays.append(arr)
        in_specs.append(pl.BlockSpec(
            (None,) * len(lead) + (rb, c),
            lambda *ids, lead=lead, last=last: lead + (jnp.minimum(step(*ids), last), 0)))
        out_specs.append(pl.BlockSpec((rb, c), lambda *ids, last=last: (jnp.minimum(step(*ids), last), 0)))
        out_shapes.append(jax.ShapeDtypeStruct((r, c), BF16))
    return arrays, in_specs, out_specs, out_shapes


def _cast_side(side_in, side_out):
    for src, dst in zip(side_in, side_out):
        dst[...] = src[...].astype(BF16)


def _ffn_kernel(x_hbm, mod_ref, wg_ref, wu_ref, wd_ref, lng_ref, lnb_ref, o_ref, h_scr, acc_scr, x_buf, x_sem,
                *, sub, alpha, nf, nm, tm):
    m = pl.program_id(0)
    f = pl.program_id(1)
    shift, scale, gate = _mod_rows(mod_ref, sub)

    def x_copy(tile):
        return pltpu.make_async_copy(x_hbm.at[pl.ds(tile * tm, tm), :], x_buf, x_sem)

    @pl.when((m == 0) & (f == 0))
    def _():
        x_copy(0).start()

    @pl.when(f == 0)
    def _():
        x_copy(m).wait()
        x = x_buf[...]
        h_scr[...] = (x * (1.0 + scale) + shift).astype(BF16)
        acc_scr[...] = alpha * x

    prefetch_step = min(1, nf - 1)

    @pl.when((f == prefetch_step) & (m + 1 < nm))
    def _():
        x_copy(m + 1).start()

    h = h_scr[...]
    g = _dot(h, wg_ref[...])
    u = _dot(h, wu_ref[...])
    a = (_silu(g) * u).astype(BF16)
    cvec = MACARON_W * (1.0 + gate)
    d = acc_scr.shape[1]
    cw = FFN_DOWN_COLS if d % FFN_DOWN_COLS == 0 else d
    for n in range(d // cw):
        cols = slice(n * cw, (n + 1) * cw)
        acc_scr[:, cols] += _dot(a, wd_ref[:, cols]) * cvec[:, cols]

    @pl.when(f == nf - 1)
    def _():
        o_ref[...] = _layer_norm(acc_scr[...], lng_ref[...], lnb_ref[...])


def _ffn(x2, mod_l, wg, wu, wd, lng, lnb, *, layer, which, sub, seq, alpha):
    t, d = x2.shape
    dff = wg.shape[-1]
    tm = min(FFN_ROWS, seq)
    tf = 512 if dff % 512 == 0 else dff
    nf = dff // tf
    nm = t // tm
    spt = seq // tm
    return pl.pallas_call(
        functools.partial(_ffn_kernel, sub=sub, alpha=alpha, nf=nf, nm=nm, tm=tm),
        out_shape=jax.ShapeDtypeStruct((t, d), F32),
        grid=(nm, nf),
        in_specs=[pl.BlockSpec(memory_space=pl.ANY),
                  pl.BlockSpec((1, 3 * N_SUB, d), lambda m, f: (m // spt, 0, 0)),
                  pl.BlockSpec((d, tf), lambda m, f: (0, f)),
                  pl.BlockSpec((d, tf), lambda m, f: (0, f)),
                  pl.BlockSpec((tf, d), lambda m, f: (f, 0)),
                  pl.BlockSpec((1, d), lambda m, f: (0, 0)),
                  pl.BlockSpec((1, d), lambda m, f: (0, 0))],
        out_specs=pl.BlockSpec((tm, d), lambda m, f: (m, 0)),
        scratch_shapes=[pltpu.VMEM((tm, d), BF16), pltpu.VMEM((tm, d), F32), pltpu.VMEM((tm, d), F32),
                        pltpu.SemaphoreType.DMA(())],
        compiler_params=pltpu.CompilerParams(dimension_semantics=("arbitrary", "arbitrary"),
                                             vmem_limit_bytes=FFN_VMEM_LIMIT),
        name=f"ffn_l{layer}_{which}",
    )(x2, mod_l, wg, wu, wd, lng, lnb)


def _proj_hg_kernel(*refs, layer, nside):
    x_ref, mod_ref, lbl_ref, wq_ref, wf_ref, wi_ref, wg_ref = refs[:7]
    q_ref, lf_ref, k_ref, v_ref, sg_ref = refs[7 + nside:12 + nside]
    _cast_side(refs[7:7 + nside], refs[12 + nside:])
    shift, scale, _ = _mod_rows(mod_ref, 1)
    h = (x_ref[...] * (1.0 + scale) + shift).astype(BF16)
    logits = lbl_ref[...]
    e = jnp.exp(logits - jnp.max(logits, axis=0, keepdims=True))
    p = e / jnp.sum(e, axis=0, keepdims=True)
    lb = jnp.zeros_like(p[0:1, :])
    for i in range(1, layer + 1):
        lb = lb + p[i:i + 1, :]
    q_ref[...] = _silu(_dot(h, wq_ref[...])).astype(BF16)
    z = _dot(h, wf_ref[...])
    ez = jnp.exp(-jnp.abs(z))
    r = 1.0 / (1.0 + ez)
    er = ez * r
    pos = z >= 0.0
    fgate = lb + (1.0 - lb) * jnp.where(pos, r, er)
    lf_ref[...] = jnp.log2(jnp.maximum(fgate, F_MIN))
    k_ref[...] = ((1.0 - lb) * jnp.where(pos, er, r)).astype(BF16)
    v_ref[...] = _dot(h, wi_ref[...]).astype(BF16)
    sg_ref[...] = _silu(_dot(h, wg_ref[...])).astype(BF16)


def _proj_hg(x2, mod_l, lb_logits, w_in, side, *, layer, seq, hgw):
    t, d = x2.shape
    tm = min(512, seq)
    spt = seq // tm
    depth = lb_logits.shape[0]
    side_arrays, side_in, side_out, side_shapes = _side_specs(side, t // tm, lambda m: m)

    def wspec(col):
        return pl.BlockSpec((d, hgw), lambda m, col=col: (0, col), pipeline_mode=pl.Buffered(1))

    out = jax.ShapeDtypeStruct((t, hgw), BF16)
    ospec = pl.BlockSpec((tm, hgw), lambda m: (m, 0))
    outs = pl.pallas_call(
        functools.partial(_proj_hg_kernel, layer=layer, nside=len(side)),
        out_shape=[out, jax.ShapeDtypeStruct((t, hgw), F32), out, out, out] + side_shapes,
        grid=(t // tm,),
        in_specs=[pl.BlockSpec((tm, d), lambda m: (m, 0)),
                  pl.BlockSpec((1, 3 * N_SUB, d), lambda m: (m // spt, 0, 0)),
                  pl.BlockSpec((depth, hgw), lambda m: (0, 0)),
                  wspec(0), wspec(1), wspec(2), wspec(3)] + side_in,
        out_specs=[ospec, ospec, ospec, ospec, ospec] + side_out,
        compiler_params=_cparams(("arbitrary",)),
        name=f"proj_hg_l{layer}",
    )(x2, mod_l, lb_logits, w_in, w_in, w_in, w_in, *side_arrays)
    return outs[:5], outs[5:]


def _proj_mb_kernel(*refs, nside):
    x_ref, mod_ref, wq_ref, wk_ref, wv_ref = refs[:5]
    q_ref, k_ref, v_ref = refs[5 + nside:8 + nside]
    _cast_side(refs[5:5 + nside], refs[8 + nside:])
    shift, scale, _ = _mod_rows(mod_ref, 1)
    h = (x_ref[...] * (1.0 + scale) + shift).astype(BF16)
    q_ref[...] = _dot(h, wq_ref[...]).astype(BF16)
    k_ref[...] = _dot(h, wk_ref[...]).astype(BF16)
    v_ref[...] = _dot(h, wv_ref[...]).astype(BF16)


def _proj_mb(x2, mod_l, w_in, side, *, layer, seq, mbw, col0):
    t, d = x2.shape
    tm = min(512, seq)
    spt = seq // tm
    side_arrays, side_in, side_out, side_shapes = _side_specs(side, t // tm, lambda m: m)

    def wspec(col):
        return pl.BlockSpec((d, mbw), lambda m, col=col: (0, col), pipeline_mode=pl.Buffered(1))

    out = jax.ShapeDtypeStruct((t, mbw), BF16)
    ospec = pl.BlockSpec((tm, mbw), lambda m: (m, 0))
    outs = pl.pallas_call(
        functools.partial(_proj_mb_kernel, nside=len(side)),
        out_shape=[out, out, out] + side_shapes,
        grid=(t // tm,),
        in_specs=[pl.BlockSpec((tm, d), lambda m: (m, 0)),
                  pl.BlockSpec((1, 3 * N_SUB, d), lambda m: (m // spt, 0, 0)),
                  wspec(col0), wspec(col0 + 1), wspec(col0 + 2)] + side_in,
        out_specs=[ospec, ospec, ospec] + side_out,
        compiler_params=_cparams(("arbitrary",)),
        name=f"proj_mb_l{layer}",
    )(x2, mod_l, w_in, w_in, w_in, *side_arrays)
    return outs[:3], outs[3:]


def _bcast_left_end(w, c, row):
    n = w.shape[0]
    if c >= 8:
        parts = []
        for g in range(n // (2 * c)):
            r = g * 2 * c + c - 1
            parts.append(jnp.broadcast_to(w[r:r + 1, :], (2 * c, w.shape[1])))
        return parts[0] if len(parts) == 1 else jnp.concatenate(parts, axis=0)
    up = lambda x, s: pltpu.roll(x, n - s, 0)
    down = lambda x, s: pltpu.roll(x, s, 0)
    if c == 1:
        return jnp.where((row & 1) == 1, down(w, 1), w)
    if c == 2:
        z = jnp.where((row & 3) == 0, up(w, 1), w)
        return jnp.where((row & 3) >= 2, down(z, 2), z)
    z = jnp.where((row & 7) == 2, up(w, 1), w)
    z = jnp.where((row & 7) < 2, up(z, 2), z)
    return jnp.where((row & 7) >= 4, down(z, 4), z)


def _hgrn_kernel(*refs, chunk, nchunk, nh, nside):
    q_ref, lf_ref, k_ref, v_ref, sg_ref, ng_ref = refs[:6]
    o_ref = refs[6 + nside]
    st_scr = refs[-1]
    _cast_side(refs[6:6 + nside], refs[7 + nside:-1])

    @pl.when(pl.program_id(2) == 0)
    def _():
        st_scr[...] = jnp.zeros_like(st_scr)

    row = lax.broadcasted_iota(jnp.int32, (chunk, HEAD), 0)
    rr = lax.broadcasted_iota(jnp.int32, (chunk, chunk), 0)
    cc = lax.broadcasted_iota(jnp.int32, (chunk, chunk), 1)
    nlev = int(math.log2(chunk))
    pair = [((rr >> (lev + 1)) == (cc >> (lev + 1))) & ((rr & (1 << lev)) != 0) & ((cc & (1 << lev)) == 0)
            for lev in range(nlev)]

    def one_head(rows, hh):
        lanes = slice(hh * HEAD, (hh + 1) * HEAD)
        w = lf_ref[rows, lanes]
        qb = q_ref[rows, lanes]
        kb = k_ref[rows, lanes]
        q = qb.astype(F32)
        k = kb.astype(F32)
        v = v_ref[rows, lanes]
        attn = jnp.where(rr == cc, _dot_nt(qb, kb), 0.0)
        for lev in range(nlev):
            c = 1 << lev
            right = (row & c) != 0
            wl = _bcast_left_end(w, c, row)
            e = jnp.exp2(jnp.where(right, w, wl - w))
            p = _dot_nt((q * e).astype(BF16), (k * e).astype(BF16))
            attn = attn + jnp.where(pair[lev], p, 0.0)
            w = jnp.where(right, w + wl, w)
        bl = w[chunk - 1:chunk, :]
        qe = (q * jnp.exp2(w)).astype(BF16)
        ke = (k * jnp.exp2(bl - w)).astype(BF16)
        st = st_scr[hh]
        o = _dot(attn.astype(BF16), v) + _dot_nt(qe, st.astype(BF16))
        st_scr[hh] = st * jnp.exp2(bl) + _dot_tn(v, ke)
        o = o * lax.rsqrt(jnp.mean(o * o, axis=-1, keepdims=True) + RMS_EPS)
        o = o * ng_ref[:, lanes] * sg_ref[rows, lanes].astype(F32)
        o_ref[rows, lanes] = o.astype(BF16)

    def body(ci, carry):
        rows = pl.ds(pl.multiple_of(ci * chunk, chunk), chunk)
        for hh in range(nh):
            one_head(rows, hh)
        return carry

    lax.fori_loop(0, nchunk, body, 0, unroll=min(HG_UNROLL, nchunk))


def _hgrn(qf, lf, kk, vv, sg, norm_g, side, *, layer, batch, seq):
    t, hgw = qf.shape
    heads = hgw // HEAD
    nh = math.gcd(heads, HG_HEADS_PER_STEP)
    lb = min(HG_BLOCK, seq)
    chunk = min(HG_CHUNK, lb)
    nsb = seq // lb
    nhp = heads // nh
    side_arrays, side_in, side_out, side_shapes = _side_specs(
        side, batch * nhp * nsb, lambda b, h, s: (b * nhp + h) * nsb + s)
    spec = pl.BlockSpec((lb, nh * HEAD), lambda b, h, s: (b * nsb + s, h))
    outs = pl.pallas_call(
        functools.partial(_hgrn_kernel, chunk=chunk, nchunk=lb // chunk, nh=nh, nside=len(side)),
        out_shape=[jax.ShapeDtypeStruct((t, hgw), BF16)] + side_shapes,
        grid=(batch, nhp, nsb),
        in_specs=[spec, spec, spec, spec, spec,
                  pl.BlockSpec((None, 1, nh * HEAD), lambda b, h, s: (layer, 0, h))] + side_in,
        out_specs=[spec] + side_out,
        scratch_shapes=[pltpu.VMEM((nh, HEAD, HEAD), F32)],
        compiler_params=_cparams(("arbitrary", "arbitrary", "arbitrary")),
        name=f"hgrn_l{layer}",
    )(qf, lf, kk, vv, sg, norm_g.reshape(norm_g.shape[0], 1, hgw), *side_arrays)
    return outs[0], outs[1:]


def _moba_kernel(*refs, nblk, blk, nqb, with_mod):
    q_ref, k_ref, v_ref = refs[:3]
    if with_mod:
        o_ref, mod_ref = refs[6:8]
        _adaln_kernel(*refs[3:6], mod_ref)
    else:
        o_ref = refs[3]
    kmh_scr, kml_scr, vt_scr = refs[-3:]
    tq = nqb * blk
    ntile = nblk // nqb
    nrow = kmh_scr.shape[0]
    seq = nblk * blk

    kmh_scr[...] = jnp.zeros_like(kmh_scr)
    kml_scr[...] = jnp.zeros_like(kml_scr)
    for j in range(nblk):
        km = jnp.sum(k_ref[j * blk:(j + 1) * blk, :].astype(F32), axis=0, keepdims=True) * (1.0 / blk)
        hi = km.astype(BF16)
        kmh_scr[j:j + 1, :] = hi
        kml_scr[j:j + 1, :] = (km - hi.astype(F32)).astype(BF16)
    ones_rows = (lax.broadcasted_iota(jnp.int32, (VT_PAD, tq), 0) == 0).astype(BF16)
    for g in range(ntile):
        vt_scr[g, :HEAD, :] = v_ref[g * tq:(g + 1) * tq, :].astype(F32).T.astype(BF16)
        vt_scr[g, HEAD:, :] = ones_rows

    scale = HEAD ** -0.5
    q_all = q_ref[...]
    gate = _dot_nt(kmh_scr[...], q_all) + _dot_nt(kml_scr[...], q_all)
    kb = lax.broadcasted_iota(jnp.int32, (nrow, seq), 0).astype(F32)
    qb = (lax.broadcasted_iota(jnp.int32, (1, seq), 1) // blk).astype(F32)
    g_ = jnp.where(kb < qb, gate, NEG)
    picks = []
    for _ in range(MB_TOPK):
        m = jnp.max(g_, axis=0, keepdims=True)
        idx = jnp.min(jnp.where(g_ == m, kb, float(nrow)), axis=0, keepdims=True)
        picks.append(jnp.where(idx < qb, idx, -1.0))
        g_ = jnp.where(kb == idx, -3e38, g_)

    c = scale * math.log2(math.e)
    tri = (lax.broadcasted_iota(jnp.int32, (blk, blk), 0) <= lax.broadcasted_iota(jnp.int32, (blk, blk), 1))
    ncb = math.gcd(nqb, MB_CHAIN_BLOCKS)
    nch = nqb // ncb
    cw = ncb * blk

    def scores(it, ch, g):
        own = g == it
        e0 = ch * ncb
        nkb = e0 + ncb if own else nqb
        q0 = (it * nqb + e0) * blk
        s = _dot_nt(k_ref[g * tq:g * tq + nkb * blk, :], q_ref[q0:q0 + cw, :])
        parts = []
        for d in range(nkb):
            sd = s[d * blk:(d + 1) * blk, :]
            j_f = float(g * nqb + d)
            cols = []
            for e in range(e0, e0 + ncb):
                sde = sd[:, (e - e0) * blk:(e - e0 + 1) * blk]
                lanes = slice((it * nqb + e) * blk, (it * nqb + e + 1) * blk)
                if own and d == e:
                    cols.append(jnp.where(tri, sde, NEG))
                elif own and d > e:
                    cols.append(jnp.full_like(sde, NEG))
                else:
                    sel = (picks[0][:, lanes] == j_f) | (picks[1][:, lanes] == j_f) | (picks[2][:, lanes] == j_f)
                    cols.append(sde + jnp.where(sel, 0.0, NEG))
            parts.append(cols[0] if ncb == 1 else jnp.concatenate(cols, axis=1))
        return parts[0] if nkb == 1 else jnp.concatenate(parts, axis=0)

    def update(g, s, carry):
        m_prev, acc = carry
        m_new = jnp.maximum(m_prev, jnp.max(s, axis=0, keepdims=True))
        a = jnp.exp2((m_prev - m_new) * c)
        p = jnp.exp2((s - m_new) * c)
        acc = a * acc + _dot(vt_scr[g, :, :s.shape[0]], p.astype(BF16))
        return m_new, acc

    steps = [(it, g) for it in range(ntile) for g in range(it + 1)]
    init = (jnp.full((1, cw), NEG, F32), jnp.zeros((HEAD + VT_PAD, cw), F32))
    cur = [scores(steps[0][0], ch, steps[0][1]) for ch in range(nch)]
    carry = [init] * nch
    for i, (it, g) in enumerate(steps):
        nxt = [scores(steps[i + 1][0], ch, steps[i + 1][1]) for ch in range(nch)] if i + 1 < len(steps) else None
        carry = [update(g, cur[ch], carry[ch]) for ch in range(nch)]
        if g == it:
            outs = [acc[:HEAD, :] / acc[HEAD:HEAD + 1, :] for _, acc in carry]
            out = outs[0] if nch == 1 else jnp.concatenate(outs, axis=1)
            o_ref[it * tq:(it + 1) * tq, :] = out.T.astype(BF16)
            carry = [init] * nch
        cur = nxt


def _moba(mq, mk, mv, ada, *, layer, batch, seq):
    t, mbw = mq.shape
    heads = mbw // HEAD
    blk = MB_BLOCK
    nblk = seq // blk
    nqb = math.gcd(nblk, MB_TILE_BLOCKS)
    assert seq % blk == 0
    nrow = -(-nblk // 16) * 16
    spec = pl.BlockSpec((seq, HEAD), lambda b, h: (b, h))
    in_specs, out_specs, out_shape, args = [spec, spec, spec], [spec], [jax.ShapeDtypeStruct((t, mbw), BF16)], []
    if ada is not None:
        c_pad, ada_w, ada_b = ada
        a_in, a_out, a_shape, _ = _adaln_specs(c_pad, ada_w, layer + 1, lambda b, h: b * heads + h, batch * heads)
        in_specs, out_specs, out_shape = in_specs + a_in, out_specs + [a_out], out_shape + [a_shape]
        args = [c_pad, ada_w, ada_b.reshape(ada_w.shape[0], 1, ada_w.shape[2])]
    outs = pl.pallas_call(
        functools.partial(_moba_kernel, nblk=nblk, blk=blk, nqb=nqb, with_mod=ada is not None),
        out_shape=out_shape,
        grid=(batch, heads),
        in_specs=in_specs,
        out_specs=out_specs,
        scratch_shapes=[pltpu.VMEM((nrow, HEAD), BF16), pltpu.VMEM((nrow, HEAD), BF16),
                        pltpu.VMEM((nblk // nqb, HEAD + VT_PAD, nqb * blk), BF16)],
        compiler_params=_cparams(("arbitrary", "arbitrary")),
        name=f"moba_l{layer}",
    )(mq, mk, mv, *args)
    return outs[0], (outs[1] if ada is not None else None)


def _merge_kernel(*refs, alpha, ng):
    x_ref, mod_ref, oa_ref, ob_ref = refs[:4]
    wga_refs, wgb_refs = refs[4:4 + ng], refs[4 + ng:4 + 2 * ng]
    wa_ref, wb_ref, wo_ref, lng_ref, lnb_ref, o_ref = refs[4 + 2 * ng:]
    shift, scale, gate = _mod_rows(mod_ref, 1)
    x = x_ref[...]
    h = (x * (1.0 + scale) + shift).astype(BF16)

    def gates(w_refs):
        parts = [jax.nn.sigmoid(_dot(h, w[...])) for w in w_refs]
        return parts[0] if ng == 1 else jnp.concatenate(parts, axis=1)

    merged = gates(wga_refs) * _dot(oa_ref[...], wa_ref[...])
    merged = merged + gates(wgb_refs) * _dot(ob_ref[...], wb_ref[...])
    y = _dot(merged.astype(BF16), wo_ref[...])
    z = alpha * x + (1.0 + gate) * y
    o_ref[...] = _layer_norm(z, lng_ref[...], lnb_ref[...])


def _merge(x2, mod_l, oa, ob, w_in, wa, wb, wo, lng, lnb, *, layer, seq, alpha):
    t, d = x2.shape
    hgw, mbw = oa.shape[1], ob.shape[1]
    tm = min(256, seq)
    spt = seq // tm
    one = pl.Buffered(1)
    gw = math.gcd(d, w_in.shape[1] - 2 * d)
    ng = d // gw
    g0 = (w_in.shape[1] - 2 * d) // gw
    gspecs = [pl.BlockSpec((d, gw), lambda m, col=g0 + i: (0, col), pipeline_mode=one) for i in range(2 * ng)]
    return pl.pallas_call(
        functools.partial(_merge_kernel, alpha=alpha, ng=ng),
        out_shape=jax.ShapeDtypeStruct((t, d), F32),
        grid=(t // tm,),
        in_specs=[pl.BlockSpec((tm, d), lambda m: (m, 0)),
                  pl.BlockSpec((1, 3 * N_SUB, d), lambda m: (m // spt, 0, 0)),
                  pl.BlockSpec((tm, hgw), lambda m: (m, 0)),
                  pl.BlockSpec((tm, mbw), lambda m: (m, 0))] + gspecs + [
                  pl.BlockSpec((hgw, d), lambda m: (0, 0), pipeline_mode=one),
                  pl.BlockSpec((mbw, d), lambda m: (0, 0), pipeline_mode=one),
                  pl.BlockSpec((d, d), lambda m: (0, 0), pipeline_mode=one),
                  pl.BlockSpec((1, d), lambda m: (0, 0)),
                  pl.BlockSpec((1, d), lambda m: (0, 0))],
        out_specs=pl.BlockSpec((tm, d), lambda m: (m, 0)),
        compiler_params=_cparams(("parallel",)),
        name=f"merge_l{layer}",
    )(x2, mod_l, oa, ob, *([w_in] * (2 * ng)), wa, wb, wo, lng, lnb)


def kernel(x, c, ada_w, ada_b, ln_g, ln_b, ffn_w_gate, ffn_w_up, ffn_w_down, w_in, hg_lb_logits,
           hg_norm_g, w_branch_a, w_branch_b, w_out):
    batch, seq, d = x.shape
    depth = ada_w.shape[0]
    hgw = hg_norm_g.shape[1]
    mbw = w_branch_b.shape[1]
    assert w_in.shape[2] == 4 * hgw + 3 * mbw + 2 * d and hgw == mbw and d % hgw == 0
    alpha = (2 * depth) ** 0.25

    c_pad = jnp.pad(c, ((0, 8 - batch % 8 if batch % 8 else 0), (0, 0)))
    mod_rows = _adaln(c_pad, ada_w, ada_b, 0)

    def ffn_w(l, j):
        return [(ffn_w_gate, (l, j)), (ffn_w_up, (l, j)), (ffn_w_down, (l, j))]

    def branch_w(l):
        return [(w_branch_a, (l,)), (w_branch_b, (l,)), (w_out, (l,))]

    ffn0 = [w[0, 0].astype(BF16) for w in (ffn_w_gate, ffn_w_up, ffn_w_down)]
    win = w_in[0].astype(BF16)
    wa, wb, wo = [w[0].astype(BF16) for w in (w_branch_a, w_branch_b, w_out)]

    x2 = x.reshape(batch * seq, d)
    for l in range(depth):
        more = l + 1 < depth
        mod_l = mod_rows[:batch].reshape(batch, 3 * N_SUB, d)
        lng = [ln_g[l, j].reshape(1, d) for j in range(N_SUB)]
        lnb = [ln_b[l, j].reshape(1, d) for j in range(N_SUB)]
        x2 = _ffn(x2, mod_l, *ffn0, lng[0], lnb[0], layer=l, which=0, sub=0, seq=seq, alpha=alpha)
        (qf, lf, kk, vv, sg), cast_h = _proj_hg(x2, mod_l, hg_lb_logits, win,
                                               ffn_w(l, 1) + (branch_w(l + 1) if more else []),
                                               layer=l, seq=seq, hgw=hgw)
        (mq, mk, mv), cast_m = _proj_mb(x2, mod_l, win, ffn_w(l + 1, 0) if more else [],
                                        layer=l, seq=seq, mbw=mbw, col0=4 * hgw // mbw)
        oa, cast_g = _hgrn(qf, lf, kk, vv, sg, hg_norm_g, [(w_in, (l + 1,))] if more else [],
                           layer=l, batch=batch, seq=seq)
        ob, mod_rows = _moba(mq, mk, mv, (c_pad, ada_w, ada_b) if more else None, layer=l, batch=batch, seq=seq)
        x2 = _merge(x2, mod_l, oa, ob, win, wa, wb, wo, lng[1], lnb[1], layer=l, seq=seq, alpha=alpha)
        x2 = _ffn(x2, mod_l, *cast_h[:3], lng[2], lnb[2], layer=l, which=1, sub=2, seq=seq, alpha=alpha)
        if more:
            ffn0, (wa, wb, wo), (win,) = cast_m, cast_h[3:], cast_g
    return x2.reshape(batch, seq, d)
```

```python
import functools
import math

import jax
import jax.numpy as jnp
from jax import lax
from jax.experimental import pallas as pl
from jax.experimental.pallas import tpu as pltpu

F32 = jnp.float32
BF16 = jnp.bfloat16

HEAD = 128
MB_BLOCK = 256
MB_TOPK = 3
MB_TILE_BLOCKS = 4
MB_CHAIN_BLOCKS = 2
VT_PAD = 16
HG_CHUNK = 128
HG_BLOCK = 1024
HG_UNROLL = 8
HG_HEADS_PER_STEP = 4
F_MIN = 1e-30
MACARON_W = 0.5
N_SUB = 3
LN_EPS = 1e-5
RMS_EPS = 1e-6
NEG = -1e30
LANES = 128
VMEM_LIMIT = 56 * 1024 * 1024
FFN_VMEM_LIMIT = 60 * 1024 * 1024
FFN_ROWS = 1024
FFN_DOWN_COLS = 512


def _cparams(sem):
    return pltpu.CompilerParams(dimension_semantics=sem, vmem_limit_bytes=VMEM_LIMIT)


def _dot(a, b):
    return jnp.dot(a, b, preferred_element_type=F32)


def _dot_nt(a, b):
    return lax.dot_general(a, b, (((1,), (1,)), ((), ())), preferred_element_type=F32)


def _dot_tn(a, b):
    return lax.dot_general(a, b, (((0,), (0,)), ((), ())), preferred_element_type=F32)


def _silu(x):
    return x * jax.nn.sigmoid(x)


def _layer_norm(z, g, b):
    mu = jnp.mean(z, axis=-1, keepdims=True)
    zc = z - mu
    var = jnp.mean(zc * zc, axis=-1, keepdims=True)
    return zc * lax.rsqrt(var + LN_EPS) * g + b


def _mod_rows(mod_ref, sub):
    return (mod_ref[0, 3 * sub:3 * sub + 1, :], mod_ref[0, 3 * sub + 1:3 * sub + 2, :],
            mod_ref[0, 3 * sub + 2:3 * sub + 3, :])


def _adaln_kernel(c_ref, w_ref, b_ref, o_ref):
    cond = _silu(c_ref[...])
    o_ref[...] = _dot(cond, w_ref[...]) + b_ref[...]


def _adaln_specs(c_pad, ada_w, layer, step, max_steps=None):
    _, d, n = ada_w.shape
    tn = min(1024, d)
    while max_steps is not None and (n % tn or n // tn > max_steps):
        tn += LANES
    last = n // tn - 1
    col = lambda *ids: jnp.minimum(step(*ids), last)
    in_specs = [pl.BlockSpec(c_pad.shape, lambda *ids: (0, 0)),
                pl.BlockSpec((None, d, tn), lambda *ids: (layer, 0, col(*ids))),
                pl.BlockSpec((None, 1, tn), lambda *ids: (layer, 0, col(*ids)))]
    out_spec = pl.BlockSpec((c_pad.shape[0], tn), lambda *ids: (0, col(*ids)))
    return in_specs, out_spec, jax.ShapeDtypeStruct((c_pad.shape[0], n), F32), n // tn


def _adaln(c_pad, ada_w, ada_b, layer):
    depth, _, n = ada_w.shape
    in_specs, out_spec, out_shape, nsteps = _adaln_specs(c_pad, ada_w, layer, lambda j: j)
    return pl.pallas_call(
        _adaln_kernel,
        out_shape=out_shape,
        grid=(nsteps,),
        in_specs=in_specs,
        out_specs=out_spec,
        compiler_params=_cparams(("arbitrary",)),
        name="adaln",
    )(c_pad, ada_w, ada_b.reshape(depth, 1, n))


def _side_rows(r, nsteps):
    rb = 16 * -(-r // (16 * nsteps))
    while rb < r and r % rb:
        rb += 16
    return min(rb, r)


def _side_specs(side, nsteps, step):
    arrays, in_specs, out_specs, out_shapes = [], [], [], []
    for arr, lead in side:
        r, c = arr.shape[-2:]
        rb = _side_rows(r, nsteps)
        last = r // rb - 1
        arrays.append(arr)
        in_specs.append(pl.BlockSpec(
            (None,) * len(lead) + (rb, c),
            lambda *ids, lead=lead, last=last: lead + (jnp.minimum(step(*ids), last), 0)))
        out_specs.append(pl.BlockSpec((rb, c), lambda *ids, last=last: (jnp.minimum(step(*ids), last), 0)))
        out_shapes.append(jax.ShapeDtypeStruct((r, c), BF16))
    return arrays, in_specs, out_specs, out_shapes


def _cast_side(side_in, side_out):
    for src, dst in zip(side_in, side_out):
        dst[...] = src[...].astype(BF16)


def _ffn_kernel(x_hbm, mod_ref, wg_ref, wu_ref, wd_ref, lng_ref, lnb_ref, o_ref, h_scr, acc_scr, x_buf, x_sem,
                *, sub, alpha, nf, nm, tm):
    m = pl.program_id(0)
    f = pl.program_id(1)
    shift, scale, gate = _mod_rows(mod_ref, sub)

    def x_copy(tile):
        return pltpu.make_async_copy(x_hbm.at[pl.ds(tile * tm, tm), :], x_buf, x_sem)

    @pl.when((m == 0) & (f == 0))
    def _():
        x_copy(0).start()

    @pl.when(f == 0)
    def _():
        x_copy(m).wait()
        x = x_buf[...]
        h_scr[...] = (x * (1.0 + scale) + shift).astype(BF16)
        acc_scr[...] = alpha * x

    prefetch_step = min(1, nf - 1)

    @pl.when((f == prefetch_step) & (m + 1 < nm))
    def _():
        x_copy(m + 1).start()

    h = h_scr[...]
    g = _dot(h, wg_ref[...])
    u = _dot(h, wu_ref[...])
    a = (_silu(g) * u).astype(BF16)
    cvec = MACARON_W * (1.0 + gate)
    d = acc_scr.shape[1]
    cw = FFN_DOWN_COLS if d % FFN_DOWN_COLS == 0 else d
    for n in range(d // cw):
        cols = slice(n * cw, (n + 1) * cw)
        acc_scr[:, cols] += _dot(a, wd_ref[:, cols]) * cvec[:, cols]

    @pl.when(f == nf - 1)
    def _():
        o_ref[...] = _layer_norm(acc_scr[...], lng_ref[...], lnb_ref[...])


def _ffn(x2, mod_l, wg, wu, wd, lng, lnb, *, layer, which, sub, seq, alpha):
    t, d = x2.shape
    dff = wg.shape[-1]
    tm = min(FFN_ROWS, seq)
    tf = 512 if dff % 512 == 0 else dff
    nf = dff // tf
    nm = t // tm
    spt = seq // tm
    return pl.pallas_call(
        functools.partial(_ffn_kernel, sub=sub, alpha=alpha, nf=nf, nm=nm, tm=tm),
        out_shape=jax.ShapeDtypeStruct((t, d), F32),
        grid=(nm, nf),
        in_specs=[pl.BlockSpec(memory_space=pl.ANY),
                  pl.BlockSpec((1, 3 * N_SUB, d), lambda m, f: (m // spt, 0, 0)),
                  pl.BlockSpec((d, tf), lambda m, f: (0, f)),
                  pl.BlockSpec((d, tf), lambda m, f: (0, f)),
                  pl.BlockSpec((tf, d), lambda m, f: (f, 0)),
                  pl.BlockSpec((1, d), lambda m, f: (0, 0)),
                  pl.BlockSpec((1, d), lambda m, f: (0, 0))],
        out_specs=pl.BlockSpec((tm, d), lambda m, f: (m, 0)),
        scratch_shapes=[pltpu.VMEM((tm, d), BF16), pltpu.VMEM((tm, d), F32), pltpu.VMEM((tm, d), F32),
                        pltpu.SemaphoreType.DMA(())],
        compiler_params=pltpu.CompilerParams(dimension_semantics=("arbitrary", "arbitrary"),
                                             vmem_limit_bytes=FFN_VMEM_LIMIT),
        name=f"ffn_l{layer}_{which}",
    )(x2, mod_l, wg, wu, wd, lng, lnb)


def _proj_hg_kernel(*refs, layer, nside):
    x_ref, mod_ref, lbl_ref, wq_ref, wf_ref, wi_ref, wg_ref = refs[:7]
    q_ref, lf_ref, k_ref, v_ref, sg_ref = refs[7 + nside:12 + nside]
    _cast_side(refs[7:7 + nside], refs[12 + nside:])
    shift, scale, _ = _mod_rows(mod_ref, 1)
    h = (x_ref[...] * (1.0 + scale) + shift).astype(BF16)
    logits = lbl_ref[...]
    e = jnp.exp(logits - jnp.max(logits, axis=0, keepdims=True))
    p = e / jnp.sum(e, axis=0, keepdims=True)
    lb = jnp.zeros_like(p[0:1, :])
    for i in range(1, layer + 1):
        lb = lb + p[i:i + 1, :]
    q_ref[...] = _silu(_dot(h, wq_ref[...])).astype(BF16)
    z = _dot(h, wf_ref[...])
    ez = jnp.exp(-jnp.abs(z))
    r = 1.0 / (1.0 + ez)
    er = ez * r
    pos = z >= 0.0
    fgate = lb + (1.0 - lb) * jnp.where(pos, r, er)
    lf_ref[...] = jnp.log2(jnp.maximum(fgate, F_MIN))
    k_ref[...] = ((1.0 - lb) * jnp.where(pos, er, r)).astype(BF16)
    v_ref[...] = _dot(h, wi_ref[...]).astype(BF16)
    sg_ref[...] = _silu(_dot(h, wg_ref[...])).astype(BF16)


def _proj_hg(x2, mod_l, lb_logits, w_in, side, *, layer, seq, hgw):
    t, d = x2.shape
    tm = min(512, seq)
    spt = seq // tm
    depth = lb_logits.shape[0]
    side_arrays, side_in, side_out, side_shapes = _side_specs(side, t // tm, lambda m: m)

    def wspec(col):
        return pl.BlockSpec((d, hgw), lambda m, col=col: (0, col), pipeline_mode=pl.Buffered(1))

    out = jax.ShapeDtypeStruct((t, hgw), BF16)
    ospec = pl.BlockSpec((tm, hgw), lambda m: (m, 0))
    outs = pl.pallas_call(
        functools.partial(_proj_hg_kernel, layer=layer, nside=len(side)),
        out_shape=[out, jax.ShapeDtypeStruct((t, hgw), F32), out, out, out] + side_shapes,
        grid=(t // tm,),
        in_specs=[pl.BlockSpec((tm, d), lambda m: (m, 0)),
                  pl.BlockSpec((1, 3 * N_SUB, d), lambda m: (m // spt, 0, 0)),
                  pl.BlockSpec((depth, hgw), lambda m: (0, 0)),
                  wspec(0), wspec(1), wspec(2), wspec(3)] + side_in,
        out_specs=[ospec, ospec, ospec, ospec, ospec] + side_out,
        compiler_params=_cparams(("arbitrary",)),
        name=f"proj_hg_l{layer}",
    )(x2, mod_l, lb_logits, w_in, w_in, w_in, w_in, *side_arrays)
    return outs[:5], outs[5:]


def _proj_mb_kernel(*refs, nside):
    x_ref, mod_ref, wq_ref, wk_ref, wv_ref = refs[:5]
    q_ref, k_ref, v_ref = refs[5 + nside:8 + nside]
    _cast_side(refs[5:5 + nside], refs[8 + nside:])
    shift, scale, _ = _mod_rows(mod_ref, 1)
    h = (x_ref[...] * (1.0 + scale) + shift).astype(BF16)
    q_ref[...] = _dot(h, wq_ref[...]).astype(BF16)
    k_ref[...] = _dot(h, wk_ref[...]).astype(BF16)
    v_ref[...] = _dot(h, wv_ref[...]).astype(BF16)


def _proj_mb(x2, mod_l, w_in, side, *, layer, seq, mbw, col0):
    t, d = x2.shape
    tm = min(512, seq)
    spt = seq // tm
    side_arrays, side_in, side_out, side_shapes = _side_specs(side, t // tm, lambda m: m)

    def wspec(col):
        return pl.BlockSpec((d, mbw), lambda m, col=col: (0, col), pipeline_mode=pl.Buffered(1))

    out = jax.ShapeDtypeStruct((t, mbw), BF16)
    ospec = pl.BlockSpec((tm, mbw), lambda m: (m, 0))
    outs = pl.pallas_call(
        functools.partial(_proj_mb_kernel, nside=len(side)),
        out_shape=[out, out, out] + side_shapes,
        grid=(t // tm,),
        in_specs=[pl.BlockSpec((tm, d), lambda m: (m, 0)),
                  pl.BlockSpec((1, 3 * N_SUB, d), lambda m: (m // spt, 0, 0)),
                  wspec(col0), wspec(col0 + 1), wspec(col0 + 2)] + side_in,
        out_specs=[ospec, ospec, ospec] + side_out,
        compiler_params=_cparams(("arbitrary",)),
        name=f"proj_mb_l{layer}",
    )(x2, mod_l, w_in, w_in, w_in, *side_arrays)
    return outs[:3], outs[3:]


def _bcast_left_end(w, c, row):
    n = w.shape[0]
    if c >= 8:
        parts = []
        for g in range(n // (2 * c)):
            r = g * 2 * c + c - 1
            parts.append(jnp.broadcast_to(w[r:r + 1, :], (2 * c, w.shape[1])))
        return parts[0] if len(parts) == 1 else jnp.concatenate(parts, axis=0)
    up = lambda x, s: pltpu.roll(x, n - s, 0)
    down = lambda x, s: pltpu.roll(x, s, 0)
    if c == 1:
        return jnp.where((row & 1) == 1, down(w, 1), w)
    if c == 2:
        z = jnp.where((row & 3) == 0, up(w, 1), w)
        return jnp.where((row & 3) >= 2, down(z, 2), z)
    z = jnp.where((row & 7) == 2, up(w, 1), w)
    z = jnp.where((row & 7) < 2, up(z, 2), z)
    return jnp.where((row & 7) >= 4, down(z, 4), z)


def _hgrn_kernel(*refs, chunk, nchunk, nh, nside):
    q_ref, lf_ref, k_ref, v_ref, sg_ref, ng_ref = refs[:6]
    o_ref = refs[6 + nside]
    st_scr = refs[-1]

    @pl.when(pl.program_id(2) == 0)
    def _():
        st_scr[...] = jnp.zeros_like(st_scr)

    _cast_side(refs[6:6 + nside], refs[7 + nside:-1])

    row = lax.broadcasted_iota(jnp.int32, (chunk, HEAD), 0)
    rr = lax.broadcasted_iota(jnp.int32, (chunk, chunk), 0)
    cc = lax.broadcasted_iota(jnp.int32, (chunk, chunk), 1)
    nlev = int(math.log2(chunk))
    pair = [((rr >> (lev + 1)) == (cc >> (lev + 1))) & ((rr & (1 << lev)) != 0) & ((cc & (1 << lev)) == 0)
            for lev in range(nlev)]

    def one_head(rows, hh):
        lanes = slice(hh * HEAD, (hh + 1) * HEAD)
        w = lf_ref[rows, lanes]
        qb = q_ref[rows, lanes]
        kb = k_ref[rows, lanes]
        q = qb.astype(F32)
        k = kb.astype(F32)
        v = v_ref[rows, lanes]
        attn = jnp.where(rr == cc, _dot_nt(qb, kb), 0.0)
        for lev in range(nlev):
            c = 1 << lev
            right = (row & c) != 0
            wl = _bcast_left_end(w, c, row)
            e = jnp.exp2(jnp.where(right, w, wl - w))
            p = _dot_nt((q * e).astype(BF16), (k * e).astype(BF16))
            attn = attn + jnp.where(pair[lev], p, 0.0)
            w = jnp.where(right, w + wl, w)
        bl = w[chunk - 1:chunk, :]
        qe = (q * jnp.exp2(w)).astype(BF16)
        ke = (k * jnp.exp2(bl - w)).astype(BF16)
        st = st_scr[hh]
        o = _dot(attn.astype(BF16), v) + _dot_nt(qe, st.astype(BF16))
        st_scr[hh] = st * jnp.exp2(bl) + _dot_tn(v, ke)
        o = o * lax.rsqrt(jnp.mean(o * o, axis=-1, keepdims=True) + RMS_EPS)
        o = o * ng_ref[:, lanes] * sg_ref[rows, lanes].astype(F32)
        o_ref[rows, lanes] = o.astype(BF16)

    def body(ci, carry):
        rows = pl.ds(pl.multiple_of(ci * chunk, chunk), chunk)
        for hh in range(nh):
            one_head(rows, hh)
        return carry

    lax.fori_loop(0, nchunk, body, 0, unroll=min(HG_UNROLL, nchunk))


def _hgrn(qf, lf, kk, vv, sg, norm_g, side, *, layer, batch, seq):
    t, hgw = qf.shape
    heads = hgw // HEAD
    nh = math.gcd(heads, HG_HEADS_PER_STEP)
    lb = min(HG_BLOCK, seq)
    chunk = min(HG_CHUNK, lb)
    nsb = seq // lb
    nhp = heads // nh
    side_arrays, side_in, side_out, side_shapes = _side_specs(
        side, batch * nhp * nsb, lambda b, h, s: (b * nhp + h) * nsb + s)
    spec = pl.BlockSpec((lb, nh * HEAD), lambda b, h, s: (b * nsb + s, h))
    outs = pl.pallas_call(
        functools.partial(_hgrn_kernel, chunk=chunk, nchunk=lb // chunk, nh=nh, nside=len(side)),
        out_shape=[jax.ShapeDtypeStruct((t, hgw), BF16)] + side_shapes,
        grid=(batch, nhp, nsb),
        in_specs=[spec, spec, spec, spec, spec,
                  pl.BlockSpec((None, 1, nh * HEAD), lambda b, h, s: (layer, 0, h))] + side_in,
        out_specs=[spec] + side_out,
        scratch_shapes=[pltpu.VMEM((nh, HEAD, HEAD), F32)],
        compiler_params=_cparams(("arbitrary", "arbitrary", "arbitrary")),
        name=f"hgrn_l{layer}",
    )(qf, lf, kk, vv, sg, norm_g.reshape(norm_g.shape[0], 1, hgw), *side_arrays)
    return outs[0], outs[1:]


def _moba_kernel(*refs, nblk, blk, nqb, with_mod):
    q_ref, k_ref, v_ref = refs[:3]
    if with_mod:
        o_ref, mod_ref = refs[6:8]
        _adaln_kernel(*refs[3:6], mod_ref)
    else:
        o_ref = refs[3]
    kmh_scr, kml_scr, vt_scr = refs[-3:]
    tq = nqb * blk
    ntile = nblk // nqb
    nrow = kmh_scr.shape[0]
    seq = nblk * blk

    kmh_scr[...] = jnp.zeros_like(kmh_scr)
    kml_scr[...] = jnp.zeros_like(kml_scr)
    for j in range(nblk):
        km = jnp.sum(k_ref[j * blk:(j + 1) * blk, :].astype(F32), axis=0, keepdims=True) * (1.0 / blk)
        hi = km.astype(BF16)
        kmh_scr[j:j + 1, :] = hi
        kml_scr[j:j + 1, :] = (km - hi.astype(F32)).astype(BF16)
    ones_rows = (lax.broadcasted_iota(jnp.int32, (VT_PAD, tq), 0) == 0).astype(BF16)
    for g in range(ntile):
        vt_scr[g, :HEAD, :] = v_ref[g * tq:(g + 1) * tq, :].astype(F32).T.astype(BF16)
        vt_scr[g, HEAD:, :] = ones_rows

    scale = HEAD ** -0.5
    q_all = q_ref[...]
    gate = _dot_nt(kmh_scr[...], q_all) + _dot_nt(kml_scr[...], q_all)
    kb = lax.broadcasted_iota(jnp.int32, (nrow, seq), 0).astype(F32)
    qb = (lax.broadcasted_iota(jnp.int32, (1, seq), 1) // blk).astype(F32)
    g_ = jnp.where(kb < qb, gate, NEG)
    picks = []
    for _ in range(MB_TOPK):
        m = jnp.max(g_, axis=0, keepdims=True)
        idx = jnp.min(jnp.where(g_ == m, kb, float(nrow)), axis=0, keepdims=True)
        picks.append(jnp.where(idx < qb, idx, -1.0))
        g_ = jnp.where(kb == idx, -3e38, g_)

    c = scale * math.log2(math.e)
    tri = (lax.broadcasted_iota(jnp.int32, (blk, blk), 0) <= lax.broadcasted_iota(jnp.int32, (blk, blk), 1))
    ncb = math.gcd(nqb, MB_CHAIN_BLOCKS)
    nch = nqb // ncb
    cw = ncb * blk

    def scores(it, ch, g):
        own = g == it
        e0 = ch * ncb
        nkb = e0 + ncb if own else nqb
        q0 = (it * nqb + e0) * blk
        s = _dot_nt(k_ref[g * tq:g * tq + nkb * blk, :], q_ref[q0:q0 + cw, :])
        parts = []
        for d in range(nkb):
            sd = s[d * blk:(d + 1) * blk, :]
            j_f = float(g * nqb + d)
            cols = []
            for e in range(e0, e0 + ncb):
                sde = sd[:, (e - e0) * blk:(e - e0 + 1) * blk]
                lanes = slice((it * nqb + e) * blk, (it * nqb + e + 1) * blk)
                if own and d == e:
                    cols.append(jnp.where(tri, sde, NEG))
                elif own and d > e:
                    cols.append(jnp.full_like(sde, NEG))
                else:
                    sel = (picks[0][:, lanes] == j_f) | (picks[1][:, lanes] == j_f) | (picks[2][:, lanes] == j_f)
                    cols.append(sde + jnp.where(sel, 0.0, NEG))
            parts.append(cols[0] if ncb == 1 else jnp.concatenate(cols, axis=1))
        return parts[0] if nkb == 1 else jnp.concatenate(parts, axis=0)

    def update(g, s, carry):
        m_prev, acc = carry
        m_new = jnp.maximum(m_prev, jnp.max(s, axis=0, keepdims=True))
        a = jnp.exp2((m_prev - m_new) * c)
        p = jnp.exp2((s - m_new) * c)
        acc = a * acc + _dot(vt_scr[g, :, :s.shape[0]], p.astype(BF16))
        return m_new, acc

    steps = [(it, g) for it in range(ntile) for g in range(it + 1)]
    init = (jnp.full((1, cw), NEG, F32), jnp.zeros((HEAD + VT_PAD, cw), F32))
    cur = [scores(steps[0][0], ch, steps[0][1]) for ch in range(nch)]
    carry = [init] * nch
    for i, (it, g) in enumerate(steps):
        nxt = [scores(steps[i + 1][0], ch, steps[i + 1][1]) for ch in range(nch)] if i + 1 < len(steps) else None
        carry = [update(g, cur[ch], carry[ch]) for ch in range(nch)]
        if g == it:
            outs = [acc[:HEAD, :] / acc[HEAD:HEAD + 1, :] for _, acc in carry]
            out = outs[0] if nch == 1 else jnp.concatenate(outs, axis=1)
            o_ref[it * tq:(it + 1) * tq, :] = out.T.astype(BF16)
            carry = [init] * nch
        cur = nxt


def _moba(mq, mk, mv, ada, *, layer, batch, seq):
    t, mbw = mq.shape
    heads = mbw // HEAD
    blk = MB_BLOCK
    nblk = seq // blk
    nqb = math.gcd(nblk, MB_TILE_BLOCKS)
    assert seq % blk == 0
    nrow = -(-nblk // 16) * 16
    spec = pl.BlockSpec((seq, HEAD), lambda b, h: (b, h))
    in_specs, out_specs, out_shape, args = [spec, spec, spec], [spec], [jax.ShapeDtypeStruct((t, mbw), BF16)], []
    if ada is not None:
        c_pad, ada_w, ada_b = ada
        a_in, a_out, a_shape, _ = _adaln_specs(c_pad, ada_w, layer + 1, lambda b, h: b * heads + h, batch * heads)
        in_specs, out_specs, out_shape = in_specs + a_in, out_specs + [a_out], out_shape + [a_shape]
        args = [c_pad, ada_w, ada_b.reshape(ada_w.shape[0], 1, ada_w.shape[2])]
    outs = pl.pallas_call(
        functools.partial(_moba_kernel, nblk=nblk, blk=blk, nqb=nqb, with_mod=ada is not None),
        out_shape=out_shape,
        grid=(batch, heads),
        in_specs=in_specs,
        out_specs=out_specs,
        scratch_shapes=[pltpu.VMEM((nrow, HEAD), BF16), pltpu.VMEM((nrow, HEAD), BF16),
                        pltpu.VMEM((nblk // nqb, HEAD + VT_PAD, nqb * blk), BF16)],
        compiler_params=_cparams(("arbitrary", "arbitrary")),
        name=f"moba_l{layer}",
    )(mq, mk, mv, *args)
    return outs[0], (outs[1] if ada is not None else None)


def _merge_kernel(*refs, alpha, ng):
    x_ref, mod_ref, oa_ref, ob_ref = refs[:4]
    wga_refs, wgb_refs = refs[4:4 + ng], refs[4 + ng:4 + 2 * ng]
    wa_ref, wb_ref, wo_ref, lng_ref, lnb_ref, o_ref = refs[4 + 2 * ng:]
    shift, scale, gate = _mod_rows(mod_ref, 1)
    x = x_ref[...]
    h = (x * (1.0 + scale) + shift).astype(BF16)

    def gates(w_refs):
        parts = [jax.nn.sigmoid(_dot(h, w[...])) for w in w_refs]
        return parts[0] if ng == 1 else jnp.concatenate(parts, axis=1)

    merged = gates(wga_refs) * _dot(oa_ref[...], wa_ref[...])
    merged = merged + gates(wgb_refs) * _dot(ob_ref[...], wb_ref[...])
    y = _dot(merged.astype(BF16), wo_ref[...])
    z = alpha * x + (1.0 + gate) * y
    o_ref[...] = _layer_norm(z, lng_ref[...], lnb_ref[...])


def _merge(x2, mod_l, oa, ob, w_in, wa, wb, wo, lng, lnb, *, layer, seq, alpha):
    t, d = x2.shape
    hgw, mbw = oa.shape[1], ob.shape[1]
    tm = min(256, seq)
    spt = seq // tm
    one = pl.Buffered(1)
    gw = math.gcd(d, w_in.shape[1] - 2 * d)
    ng = d // gw
    g0 = (w_in.shape[1] - 2 * d) // gw
    gspecs = [pl.BlockSpec((d, gw), lambda m, col=g0 + i: (0, col), pipeline_mode=one) for i in range(2 * ng)]
    return pl.pallas_call(
        functools.partial(_merge_kernel, alpha=alpha, ng=ng),
        out_shape=jax.ShapeDtypeStruct((t, d), F32),
        grid=(t // tm,),
        in_specs=[pl.BlockSpec((tm, d), lambda m: (m, 0)),
                  pl.BlockSpec((1, 3 * N_SUB, d), lambda m: (m // spt, 0, 0)),
                  pl.BlockSpec((tm, hgw), lambda m: (m, 0)),
                  pl.BlockSpec((tm, mbw), lambda m: (m, 0))] + gspecs + [
                  pl.BlockSpec((hgw, d), lambda m: (0, 0), pipeline_mode=one),
                  pl.BlockSpec((mbw, d), lambda m: (0, 0), pipeline_mode=one),
                  pl.BlockSpec((d, d), lambda m: (0, 0), pipeline_mode=one),
                  pl.BlockSpec((1, d), lambda m: (0, 0)),
                  pl.BlockSpec((1, d), lambda m: (0, 0))],
        out_specs=pl.BlockSpec((tm, d), lambda m: (m, 0)),
        compiler_params=_cparams(("parallel",)),
        name=f"merge_l{layer}",
    )(x2, mod_l, oa, ob, *([w_in] * (2 * ng)), wa, wb, wo, lng, lnb)


def kernel(x, c, ada_w, ada_b, ln_g, ln_b, ffn_w_gate, ffn_w_up, ffn_w_down, w_in, hg_lb_logits,
           hg_norm_g, w_branch_a, w_branch_b, w_out):
    batch, seq, d = x.shape
    depth = ada_w.shape[0]
    hgw = hg_norm_g.shape[1]
    mbw = w_branch_b.shape[1]
    assert w_in.shape[2] == 4 * hgw + 3 * mbw + 2 * d and hgw == mbw and d % hgw == 0
    alpha = (2 * depth) ** 0.25

    c_pad = jnp.pad(c, ((0, 8 - batch % 8 if batch % 8 else 0), (0, 0)))
    mod_rows = _adaln(c_pad, ada_w, ada_b, 0)

    def ffn_w(l, j):
        return [(ffn_w_gate, (l, j)), (ffn_w_up, (l, j)), (ffn_w_down, (l, j))]

    def branch_w(l):
        return [(w_branch_a, (l,)), (w_branch_b, (l,)), (w_out, (l,))]

    ffn0 = [w[0, 0].astype(BF16) for w in (ffn_w_gate, ffn_w_up, ffn_w_down)]
    win = w_in[0].astype(BF16)
    wa, wb, wo = [w[0].astype(BF16) for w in (w_branch_a, w_branch_b, w_out)]

    x2 = x.reshape(batch * seq, d)
    for l in range(depth):
        more = l + 1 < depth
        mod_l = mod_rows[:batch].reshape(batch, 3 * N_SUB, d)
        lng = [ln_g[l, j].reshape(1, d) for j in range(N_SUB)]
        lnb = [ln_b[l, j].reshape(1, d) for j in range(N_SUB)]
        x2 = _ffn(x2, mod_l, *ffn0, lng[0], lnb[0], layer=l, which=0, sub=0, seq=seq, alpha=alpha)
        (qf, lf, kk, vv, sg), cast_h = _proj_hg(x2, mod_l, hg_lb_logits, win,
                                               ffn_w(l, 1) + (branch_w(l + 1) if more else []),
                                               layer=l, seq=seq, hgw=hgw)
        (mq, mk, mv), cast_m = _proj_mb(x2, mod_l, win, ffn_w(l + 1, 0) if more else [],
                                        layer=l, seq=seq, mbw=mbw, col0=4 * hgw // mbw)
        oa, cast_g = _hgrn(qf, lf, kk, vv, sg, hg_norm_g, [(w_in, (l + 1,))] if more else [],
                           layer=l, batch=batch, seq=seq)
        ob, mod_rows = _moba(mq, mk, mv, (c_pad, ada_w, ada_b) if more else None, layer=l, batch=batch, seq=seq)
        x2 = _merge(x2, mod_l, oa, ob, win, wa, wb, wo, lng[1], lnb[1], layer=l, seq=seq, alpha=alpha)
        x2 = _ffn(x2, mod_l, *cast_h[:3], lng[2], lnb[2], layer=l, which=1, sub=2, seq=seq, alpha=alpha)
        if more:
            ffn0, (wa, wb, wo), (win,) = cast_m, cast_h[3:], cast_g
    return x2.reshape(batch, seq, d)
```

```python
import functools
import math

import jax
import jax.numpy as jnp
from jax import lax
from jax.experimental import pallas as pl
from jax.experimental.pallas import tpu as pltpu

F32 = jnp.float32
BF16 = jnp.bfloat16

HEAD = 128
MB_BLOCK = 256
MB_TOPK = 3
MB_TILE_BLOCKS = 2
MB_CHAIN_BLOCKS = 2
VT_PAD = 16
HG_CHUNK = 128
HG_BLOCK = 1024
HG_UNROLL = 8
HG_HEADS_PER_STEP = 4
F_MIN = 1e-30
MACARON_W = 0.5
N_SUB = 3
LN_EPS = 1e-5
RMS_EPS = 1e-6
NEG = -1e30
LANES = 128
VMEM_LIMIT = 56 * 1024 * 1024
FFN_VMEM_LIMIT = 60 * 1024 * 1024
FFN_ROWS = 1024
FFN_DOWN_COLS = 512


def _cparams(sem):
    return pltpu.CompilerParams(dimension_semantics=sem, vmem_limit_bytes=VMEM_LIMIT)


def _dot(a, b):
    return jnp.dot(a, b, preferred_element_type=F32)


def _dot_nt(a, b):
    return lax.dot_general(a, b, (((1,), (1,)), ((), ())), preferred_element_type=F32)


def _dot_tn(a, b):
    return lax.dot_general(a, b, (((0,), (0,)), ((), ())), preferred_element_type=F32)


def _silu(x):
    return x * jax.nn.sigmoid(x)


def _layer_norm(z, g, b):
    mu = jnp.mean(z, axis=-1, keepdims=True)
    zc = z - mu
    var = jnp.mean(zc * zc, axis=-1, keepdims=True)
    return zc * lax.rsqrt(var + LN_EPS) * g + b


def _mod_rows(mod_ref, sub):
    return (mod_ref[0, 3 * sub:3 * sub + 1, :], mod_ref[0, 3 * sub + 1:3 * sub + 2, :],
            mod_ref[0, 3 * sub + 2:3 * sub + 3, :])


def _adaln_kernel(c_ref, w_ref, b_ref, o_ref):
    cond = _silu(c_ref[...])
    o_ref[...] = _dot(cond, w_ref[...]) + b_ref[...]


def _adaln_specs(c_pad, ada_w, layer, step, max_steps=None):
    _, d, n = ada_w.shape
    tn = min(1024, d)
    while max_steps is not None and (n % tn or n // tn > max_steps):
        tn += LANES
    last = n // tn - 1
    col = lambda *ids: jnp.minimum(step(*ids), last)
    in_specs = [pl.BlockSpec(c_pad.shape, lambda *ids: (0, 0)),
                pl.BlockSpec((None, d, tn), lambda *ids: (layer, 0, col(*ids))),
                pl.BlockSpec((None, 1, tn), lambda *ids: (layer, 0, col(*ids)))]
    out_spec = pl.BlockSpec((c_pad.shape[0], tn), lambda *ids: (0, col(*ids)))
    return in_specs, out_spec, jax.ShapeDtypeStruct((c_pad.shape[0], n), F32), n // tn


def _adaln(c_pad, ada_w, ada_b, layer):
    depth, _, n = ada_w.shape
    in_specs, out_spec, out_shape, nsteps = _adaln_specs(c_pad, ada_w, layer, lambda j: j)
    return pl.pallas_call(
        _adaln_kernel,
        out_shape=out_shape,
        grid=(nsteps,),
        in_specs=in_specs,
        out_specs=out_spec,
        compiler_params=_cparams(("arbitrary",)),
        name="adaln",
    )(c_pad, ada_w, ada_b.reshape(depth, 1, n))


def _side_rows(r, nsteps):
    rb = 16 * -(-r // (16 * nsteps))
    while rb < r and r % rb:
        rb += 16
    return min(rb, r)


def _side_specs(side, nsteps, step):
    arrays, in_specs, out_specs, out_shapes = [], [], [], []
    for arr, lead in side:
        r, c = arr.shape[-2:]
        rb = _side_rows(r, nsteps)
        last = r // rb - 1
        arrays.append(arr)
        in_specs.append(pl.BlockSpec(
            (None,) * len(lead) + (rb, c),
            lambda *ids, lead=lead, last=last: lead + (jnp.minimum(step(*ids), last), 0)))
        out_specs.append(pl.BlockSpec((rb, c), lambda *ids, last=last: (jnp.minimum(step(*ids), last), 0)))
        out_shapes.append(jax.ShapeDtypeStruct((r, c), BF16))
    return arrays, in_specs, out_specs, out_shapes


def _cast_side(side_in, side_out):
    for src, dst in zip(side_in, side_out):
        dst[...] = src[...].astype(BF16)


def _ffn_kernel(x_hbm, mod_ref, wg_ref, wu_ref, wd_ref, lng_ref, lnb_ref, o_ref, h_scr, acc_scr, x_buf, x_sem,
                *, sub, alpha, nf, nm, tm):
    m = pl.program_id(0)
    f = pl.program_id(1)
    shift, scale, gate = _mod_rows(mod_ref, sub)

    def x_copy(tile):
        return pltpu.make_async_copy(x_hbm.at[pl.ds(tile * tm, tm), :], x_buf, x_sem)

    @pl.when((m == 0) & (f == 0))
    def _():
        x_copy(0).start()

    @pl.when(f == 0)
    def _():
        x_copy(m).wait()
        x = x_buf[...]
        h_scr[...] = (x * (1.0 + scale) + shift).astype(BF16)
        acc_scr[...] = alpha * x

    prefetch_step = min(1, nf - 1)

    @pl.when((f == prefetch_step) & (m + 1 < nm))
    def _():
        x_copy(m + 1).start()

    h = h_scr[...]
    g = _dot(h, wg_ref[...])
    u = _dot(h, wu_ref[...])
    a = (_silu(g) * u).astype(BF16)
    cvec = MACARON_W * (1.0 + gate)
    d = acc_scr.shape[1]
    cw = FFN_DOWN_COLS if d % FFN_DOWN_COLS == 0 else d
    for n in range(d // cw):
        cols = slice(n * cw, (n + 1) * cw)
        acc_scr[:, cols] += _dot(a, wd_ref[:, cols]) * cvec[:, cols]

    @pl.when(f == nf - 1)
    def _():
        o_ref[...] = _layer_norm(acc_scr[...], lng_ref[...], lnb_ref[...])


def _ffn(x2, mod_l, wg, wu, wd, lng, lnb, *, layer, which, sub, seq, alpha):
    t, d = x2.shape
    dff = wg.shape[-1]
    tm = min(FFN_ROWS, seq)
    tf = 512 if dff % 512 == 0 else dff
    nf = dff // tf
    nm = t // tm
    spt = seq // tm
    return pl.pallas_call(
        functools.partial(_ffn_kernel, sub=sub, alpha=alpha, nf=nf, nm=nm, tm=tm),
        out_shape=jax.ShapeDtypeStruct((t, d), F32),
        grid=(nm, nf),
        in_specs=[pl.BlockSpec(memory_space=pl.ANY),
                  pl.BlockSpec((1, 3 * N_SUB, d), lambda m, f: (m // spt, 0, 0)),
                  pl.BlockSpec((d, tf), lambda m, f: (0, f)),
                  pl.BlockSpec((d, tf), lambda m, f: (0, f)),
                  pl.BlockSpec((tf, d), lambda m, f: (f, 0)),
                  pl.BlockSpec((1, d), lambda m, f: (0, 0)),
                  pl.BlockSpec((1, d), lambda m, f: (0, 0))],
        out_specs=pl.BlockSpec((tm, d), lambda m, f: (m, 0)),
        scratch_shapes=[pltpu.VMEM((tm, d), BF16), pltpu.VMEM((tm, d), F32), pltpu.VMEM((tm, d), F32),
                        pltpu.SemaphoreType.DMA(())],
        compiler_params=pltpu.CompilerParams(dimension_semantics=("arbitrary", "arbitrary"),
                                             vmem_limit_bytes=FFN_VMEM_LIMIT),
        name=f"ffn_l{layer}_{which}",
    )(x2, mod_l, wg, wu, wd, lng, lnb)


def _proj_hg_kernel(*refs, layer, nside):
    x_ref, mod_ref, lbl_ref, wq_ref, wf_ref, wi_ref, wg_ref = refs[:7]
    q_ref, lf_ref, k_ref, v_ref, sg_ref = refs[7 + nside:12 + nside]
    _cast_side(refs[7:7 + nside], refs[12 + nside:])
    shift, scale, _ = _mod_rows(mod_ref, 1)
    h = (x_ref[...] * (1.0 + scale) + shift).astype(BF16)
    logits = lbl_ref[...]
    e = jnp.exp(logits - jnp.max(logits, axis=0, keepdims=True))
    p = e / jnp.sum(e, axis=0, keepdims=True)
    lb = jnp.zeros_like(p[0:1, :])
    for i in range(1, layer + 1):
        lb = lb + p[i:i + 1, :]
    q_ref[...] = _silu(_dot(h, wq_ref[...])).astype(BF16)
    z = _dot(h, wf_ref[...])
    ez = jnp.exp(-jnp.abs(z))
    r = 1.0 / (1.0 + ez)
    er = ez * r
    pos = z >= 0.0
    fgate = lb + (1.0 - lb) * jnp.where(pos, r, er)
    lf_ref[...] = jnp.log2(jnp.maximum(fgate, F_MIN))
    k_ref[...] = ((1.0 - lb) * jnp.where(pos, er, r)).astype(BF16)
    v_ref[...] = _dot(h, wi_ref[...]).astype(BF16)
    sg_ref[...] = _silu(_dot(h, wg_ref[...])).astype(BF16)


def _proj_hg(x2, mod_l, lb_logits, w_in, side, *, layer, seq, hgw):
    t, d = x2.shape
    tm = min(512, seq)
    spt = seq // tm
    depth = lb_logits.shape[0]
    side_arrays, side_in, side_out, side_shapes = _side_specs(side, t // tm, lambda m: m)

    def wspec(col):
        return pl.BlockSpec((d, hgw), lambda m, col=col: (0, col), pipeline_mode=pl.Buffered(1))

    out = jax.ShapeDtypeStruct((t, hgw), BF16)
    ospec = pl.BlockSpec((tm, hgw), lambda m: (m, 0))
    outs = pl.pallas_call(
        functools.partial(_proj_hg_kernel, layer=layer, nside=len(side)),
        out_shape=[out, jax.ShapeDtypeStruct((t, hgw), F32), out, out, out] + side_shapes,
        grid=(t // tm,),
        in_specs=[pl.BlockSpec((tm, d), lambda m: (m, 0)),
                  pl.BlockSpec((1, 3 * N_SUB, d), lambda m: (m // spt, 0, 0)),
                  pl.BlockSpec((depth, hgw), lambda m: (0, 0)),
                  wspec(0), wspec(1), wspec(2), wspec(3)] + side_in,
        out_specs=[ospec, ospec, ospec, ospec, ospec] + side_out,
        compiler_params=_cparams(("arbitrary",)),
        name=f"proj_hg_l{layer}",
    )(x2, mod_l, lb_logits, w_in, w_in, w_in, w_in, *side_arrays)
    return outs[:5], outs[5:]


def _proj_mb_kernel(*refs, nside):
    x_ref, mod_ref, wq_ref, wk_ref, wv_ref = refs[:5]
    q_ref, k_ref, v_ref = refs[5 + nside:8 + nside]
    _cast_side(refs[5:5 + nside], refs[8 + nside:])
    shift, scale, _ = _mod_rows(mod_ref, 1)
    h = (x_ref[...] * (1.0 + scale) + shift).astype(BF16)
    q_ref[...] = _dot(h, wq_ref[...]).astype(BF16)
    k_ref[...] = _dot(h, wk_ref[...]).astype(BF16)
    v_ref[...] = _dot(h, wv_ref[...]).astype(BF16)


def _proj_mb(x2, mod_l, w_in, side, *, layer, seq, mbw, col0):
    t, d = x2.shape
    tm = min(512, seq)
    spt = seq // tm
    side_arrays, side_in, side_out, side_shapes = _side_specs(side, t // tm, lambda m: m)

    def wspec(col):
        return pl.BlockSpec((d, mbw), lambda m, col=col: (0, col), pipeline_mode=pl.Buffered(1))

    out = jax.ShapeDtypeStruct((t, mbw), BF16)
    ospec = pl.BlockSpec((tm, mbw), lambda m: (m, 0))
    outs = pl.pallas_call(
        functools.partial(_proj_mb_kernel, nside=len(side)),
        out_shape=[out, out, out] + side_shapes,
        grid=(t // tm,),
        in_specs=[pl.BlockSpec((tm, d), lambda m: (m, 0)),
                  pl.BlockSpec((1, 3 * N_SUB, d), lambda m: (m // spt, 0, 0)),
                  wspec(col0), wspec(col0 + 1), wspec(col0 + 2)] + side_in,
        out_specs=[ospec, ospec, ospec] + side_out,
        compiler_params=_cparams(("arbitrary",)),
        name=f"proj_mb_l{layer}",
    )(x2, mod_l, w_in, w_in, w_in, *side_arrays)
    return outs[:3], outs[3:]


def _bcast_left_end(w, c, row):
    n = w.shape[0]
    if c >= 8:
        parts = []
        for g in range(n // (2 * c)):
            r = g * 2 * c + c - 1
            parts.append(jnp.broadcast_to(w[r:r + 1, :], (2 * c, w.shape[1])))
        return parts[0] if len(parts) == 1 else jnp.concatenate(parts, axis=0)
    up = lambda x, s: pltpu.roll(x, n - s, 0)
    down = lambda x, s: pltpu.roll(x, s, 0)
    if c == 1:
        return jnp.where((row & 1) == 1, down(w, 1), w)
    if c == 2:
        z = jnp.where((row & 3) == 0, up(w, 1), w)
        return jnp.where((row & 3) >= 2, down(z, 2), z)
    z = jnp.where((row & 7) == 2, up(w, 1), w)
    z = jnp.where((row & 7) < 2, up(z, 2), z)
    return jnp.where((row & 7) >= 4, down(z, 4), z)


def _hgrn_kernel(*refs, chunk, nchunk, nh, nside):
    q_ref, lf_ref, k_ref, v_ref, sg_ref, ng_ref = refs[:6]
    o_ref = refs[6 + nside]
    st_scr = refs[-1]

    @pl.when(pl.program_id(2) == 0)
    def _():
        st_scr[...] = jnp.zeros_like(st_scr)

    _cast_side(refs[6:6 + nside], refs[7 + nside:-1])

    row = lax.broadcasted_iota(jnp.int32, (chunk, HEAD), 0)
    rr = lax.broadcasted_iota(jnp.int32, (chunk, chunk), 0)
    cc = lax.broadcasted_iota(jnp.int32, (chunk, chunk), 1)
    nlev = int(math.log2(chunk))
    pair = [((rr >> (lev + 1)) == (cc >> (lev + 1))) & ((rr & (1 << lev)) != 0) & ((cc & (1 << lev)) == 0)
            for lev in range(nlev)]

    def one_head(rows, hh):
        lanes = slice(hh * HEAD, (hh + 1) * HEAD)
        w = lf_ref[rows, lanes]
        qb = q_ref[rows, lanes]
        kb = k_ref[rows, lanes]
        q = qb.astype(F32)
        k = kb.astype(F32)
        v = v_ref[rows, lanes]
        attn = jnp.where(rr == cc, _dot_nt(qb, kb), 0.0)
        for lev in range(nlev):
            c = 1 << lev
            right = (row & c) != 0
            wl = _bcast_left_end(w, c, row)
            e = jnp.exp2(jnp.where(right, w, wl - w))
            p = _dot_nt((q * e).astype(BF16), (k * e).astype(BF16))
            attn = attn + jnp.where(pair[lev], p, 0.0)
            w = jnp.where(right, w + wl, w)
        bl = w[chunk - 1:chunk, :]
        qe = (q * jnp.exp2(w)).astype(BF16)
        ke = (k * jnp.exp2(bl - w)).astype(BF16)
        st = st_scr[hh]
        o = _dot(attn.astype(BF16), v) + _dot_nt(qe, st.astype(BF16))
        st_scr[hh] = st * jnp.exp2(bl) + _dot_tn(v, ke)
        o = o * lax.rsqrt(jnp.mean(o * o, axis=-1, keepdims=True) + RMS_EPS)
        o = o * ng_ref[:, lanes] * sg_ref[rows, lanes].astype(F32)
        o_ref[rows, lanes] = o.astype(BF16)

    def body(ci, carry):
        rows = pl.ds(pl.multiple_of(ci * chunk, chunk), chunk)
        for hh in range(nh):
            one_head(rows, hh)
        return carry

    lax.fori_loop(0, nchunk, body, 0, unroll=min(HG_UNROLL, nchunk))


def _hgrn(qf, lf, kk, vv, sg, norm_g, side, *, layer, batch, seq):
    t, hgw = qf.shape
    heads = hgw // HEAD
    nh = math.gcd(heads, HG_HEADS_PER_STEP)
    lb = min(HG_BLOCK, seq)
    chunk = min(HG_CHUNK, lb)
    nsb = seq // lb
    nhp = heads // nh
    side_arrays, side_in, side_out, side_shapes = _side_specs(
        side, batch * nhp * nsb, lambda b, h, s: (b * nhp + h) * nsb + s)
    spec = pl.BlockSpec((lb, nh * HEAD), lambda b, h, s: (b * nsb + s, h))
    outs = pl.pallas_call(
        functools.partial(_hgrn_kernel, chunk=chunk, nchunk=lb // chunk, nh=nh, nside=len(side)),
        out_shape=[jax.ShapeDtypeStruct((t, hgw), BF16)] + side_shapes,
        grid=(batch, nhp, nsb),
        in_specs=[spec, spec, spec, spec, spec,
                  pl.BlockSpec((None, 1, nh * HEAD), lambda b, h, s: (layer, 0, h))] + side_in,
        out_specs=[spec] + side_out,
        scratch_shapes=[pltpu.VMEM((nh, HEAD, HEAD), F32)],
        compiler_params=_cparams(("arbitrary", "arbitrary", "arbitrary")),
        name=f"hgrn_l{layer}",
    )(qf, lf, kk, vv, sg, norm_g.reshape(norm_g.shape[0], 1, hgw), *side_arrays)
    return outs[0], outs[1:]


def _moba_kernel(*refs, nblk, blk, nqb, with_mod):
    q_ref, k_ref, v_ref = refs[:3]
    if with_mod:
        o_ref, mod_ref = refs[6:8]
        _adaln_kernel(*refs[3:6], mod_ref)
    else:
        o_ref = refs[3]
    kmh_scr, kml_scr, vt_scr = refs[-3:]
    tq = nqb * blk
    ntile = nblk // nqb
    nrow = kmh_scr.shape[0]
    seq = nblk * blk

    kmh_scr[...] = jnp.zeros_like(kmh_scr)
    kml_scr[...] = jnp.zeros_like(kml_scr)
    for j in range(nblk):
        km = jnp.sum(k_ref[j * blk:(j + 1) * blk, :].astype(F32), axis=0, keepdims=True) * (1.0 / blk)
        hi = km.astype(BF16)
        kmh_scr[j:j + 1, :] = hi
        kml_scr[j:j + 1, :] = (km - hi.astype(F32)).astype(BF16)
    ones_rows = (lax.broadcasted_iota(jnp.int32, (VT_PAD, tq), 0) == 0).astype(BF16)
    for g in range(ntile):
        vt_scr[g, :HEAD, :] = v_ref[g * tq:(g + 1) * tq, :].astype(F32).T.astype(BF16)
        vt_scr[g, HEAD:, :] = ones_rows

    scale = HEAD ** -0.5
    q_all = q_ref[...]
    gate = _dot_nt(kmh_scr[...], q_all) + _dot_nt(kml_scr[...], q_all)
    kb = lax.broadcasted_iota(jnp.int32, (nrow, seq), 0).astype(F32)
    qb = (lax.broadcasted_iota(jnp.int32, (1, seq), 1) // blk).astype(F32)
    g_ = jnp.where(kb < qb, gate, NEG)
    picks = []
    for _ in range(MB_TOPK):
        m = jnp.max(g_, axis=0, keepdims=True)
        idx = jnp.min(jnp.where(g_ == m, kb, float(nrow)), axis=0, keepdims=True)
        picks.append(jnp.where(idx < qb, idx, -1.0))
        g_ = jnp.where(kb == idx, -3e38, g_)

    c = scale * math.log2(math.e)
    tri = (lax.broadcasted_iota(jnp.int32, (blk, blk), 0) <= lax.broadcasted_iota(jnp.int32, (blk, blk), 1))
    ncb = math.gcd(nqb, MB_CHAIN_BLOCKS)
    nch = nqb // ncb
    cw = ncb * blk

    def scores(it, ch, g):
        own = g == it
        e0 = ch * ncb
        nkb = e0 + ncb if own else nqb
        q0 = (it * nqb + e0) * blk
        s = _dot_nt(k_ref[g * tq:g * tq + nkb * blk, :], q_ref[q0:q0 + cw, :])
        parts = []
        for d in range(nkb):
            sd = s[d * blk:(d + 1) * blk, :]
            j_f = float(g * nqb + d)
            cols = []
            for e in range(e0, e0 + ncb):
                sde = sd[:, (e - e0) * blk:(e - e0 + 1) * blk]
                lanes = slice((it * nqb + e) * blk, (it * nqb + e + 1) * blk)
                if own and d == e:
                    cols.append(jnp.where(tri, sde, NEG))
                elif own and d > e:
                    cols.append(jnp.full_like(sde, NEG))
                else:
                    sel = (picks[0][:, lanes] == j_f) | (picks[1][:, lanes] == j_f) | (picks[2][:, lanes] == j_f)
                    cols.append(sde + jnp.where(sel, 0.0, NEG))
            parts.append(cols[0] if ncb == 1 else jnp.concatenate(cols, axis=1))
        return parts[0] if nkb == 1 else jnp.concatenate(parts, axis=0)

    def update(g, s, carry):
        m_prev, acc = carry
        m_new = jnp.maximum(m_prev, jnp.max(s, axis=0, keepdims=True))
        a = jnp.exp2((m_prev - m_new) * c)
        p = jnp.exp2((s - m_new) * c)
        acc = a * acc + _dot(vt_scr[g, :, :s.shape[0]], p.astype(BF16))
        return m_new, acc

    steps = [(it, g) for it in range(ntile) for g in range(it + 1)]
    init = (jnp.full((1, cw), NEG, F32), jnp.zeros((HEAD + VT_PAD, cw), F32))
    cur = [scores(steps[0][0], ch, steps[0][1]) for ch in range(nch)]
    carry = [init] * nch
    for i, (it, g) in enumerate(steps):
        nxt = [scores(steps[i + 1][0], ch, steps[i + 1][1]) for ch in range(nch)] if i + 1 < len(steps) else None
        carry = [update(g, cur[ch], carry[ch]) for ch in range(nch)]
        if g == it:
            outs = [acc[:HEAD, :] / acc[HEAD:HEAD + 1, :] for _, acc in carry]
            out = outs[0] if nch == 1 else jnp.concatenate(outs, axis=1)
            o_ref[it * tq:(it + 1) * tq, :] = out.T.astype(BF16)
            carry = [init] * nch
        cur = nxt


def _moba(mq, mk, mv, ada, *, layer, batch, seq):
    t, mbw = mq.shape
    heads = mbw // HEAD
    blk = MB_BLOCK
    nblk = seq // blk
    nqb = math.gcd(nblk, MB_TILE_BLOCKS)
    assert seq % blk == 0
    nrow = -(-nblk // 16) * 16
    spec = pl.BlockSpec((seq, HEAD), lambda b, h: (b, h))
    in_specs, out_specs, out_shape, args = [spec, spec, spec], [spec], [jax.ShapeDtypeStruct((t, mbw), BF16)], []
    if ada is not None:
        c_pad, ada_w, ada_b = ada
        a_in, a_out, a_shape, _ = _adaln_specs(c_pad, ada_w, layer + 1, lambda b, h: b * heads + h, batch * heads)
        in_specs, out_specs, out_shape = in_specs + a_in, out_specs + [a_out], out_shape + [a_shape]
        args = [c_pad, ada_w, ada_b.reshape(ada_w.shape[0], 1, ada_w.shape[2])]
    outs = pl.pallas_call(
        functools.partial(_moba_kernel, nblk=nblk, blk=blk, nqb=nqb, with_mod=ada is not None),
        out_shape=out_shape,
        grid=(batch, heads),
        in_specs=in_specs,
        out_specs=out_specs,
        scratch_shapes=[pltpu.VMEM((nrow, HEAD), BF16), pltpu.VMEM((nrow, HEAD), BF16),
                        pltpu.VMEM((nblk // nqb, HEAD + VT_PAD, nqb * blk), BF16)],
        compiler_params=_cparams(("arbitrary", "arbitrary")),
        name=f"moba_l{layer}",
    )(mq, mk, mv, *args)
    return outs[0], (outs[1] if ada is not None else None)


def _merge_kernel(*refs, alpha, ng):
    x_ref, mod_ref, oa_ref, ob_ref = refs[:4]
    wga_refs, wgb_refs = refs[4:4 + ng], refs[4 + ng:4 + 2 * ng]
    wa_ref, wb_ref, wo_ref, lng_ref, lnb_ref, o_ref = refs[4 + 2 * ng:]
    shift, scale, gate = _mod_rows(mod_ref, 1)
    x = x_ref[...]
    h = (x * (1.0 + scale) + shift).astype(BF16)

    def gates(w_refs):
        parts = [jax.nn.sigmoid(_dot(h, w[...])) for w in w_refs]
        return parts[0] if ng == 1 else jnp.concatenate(parts, axis=1)

    merged = gates(wga_refs) * _dot(oa_ref[...], wa_ref[...])
    merged = merged + gates(wgb_refs) * _dot(ob_ref[...], wb_ref[...])
    y = _dot(merged.astype(BF16), wo_ref[...])
    z = alpha * x + (1.0 + gate) * y
    o_ref[...] = _layer_norm(z, lng_ref[...], lnb_ref[...])


def _merge(x2, mod_l, oa, ob, w_in, wa, wb, wo, lng, lnb, *, layer, seq, alpha):
    t, d = x2.shape
    hgw, mbw = oa.shape[1], ob.shape[1]
    tm = min(256, seq)
    spt = seq // tm
    one = pl.Buffered(1)
    gw = math.gcd(d, w_in.shape[1] - 2 * d)
    ng = d // gw
    g0 = (w_in.shape[1] - 2 * d) // gw
    gspecs = [pl.BlockSpec((d, gw), lambda m, col=g0 + i: (0, col), pipeline_mode=one) for i in range(2 * ng)]
    return pl.pallas_call(
        functools.partial(_merge_kernel, alpha=alpha, ng=ng),
        out_shape=jax.ShapeDtypeStruct((t, d), F32),
        grid=(t // tm,),
        in_specs=[pl.BlockSpec((tm, d), lambda m: (m, 0)),
                  pl.BlockSpec((1, 3 * N_SUB, d), lambda m: (m // spt, 0, 0)),
                  pl.BlockSpec((tm, hgw), lambda m: (m, 0)),
                  pl.BlockSpec((tm, mbw), lambda m: (m, 0))] + gspecs + [
                  pl.BlockSpec((hgw, d), lambda m: (0, 0), pipeline_mode=one),
                  pl.BlockSpec((mbw, d), lambda m: (0, 0), pipeline_mode=one),
                  pl.BlockSpec((d, d), lambda m: (0, 0), pipeline_mode=one),
                  pl.BlockSpec((1, d), lambda m: (0, 0)),
                  pl.BlockSpec((1, d), lambda m: (0, 0))],
        out_specs=pl.BlockSpec((tm, d), lambda m: (m, 0)),
        compiler_params=_cparams(("parallel",)),
        name=f"merge_l{layer}",
    )(x2, mod_l, oa, ob, *([w_in] * (2 * ng)), wa, wb, wo, lng, lnb)


def kernel(x, c, ada_w, ada_b, ln_g, ln_b, ffn_w_gate, ffn_w_up, ffn_w_down, w_in, hg_lb_logits,
           hg_norm_g, w_branch_a, w_branch_b, w_out):
    batch, seq, d = x.shape
    depth = ada_w.shape[0]
    hgw = hg_norm_g.shape[1]
    mbw = w_branch_b.shape[1]
    assert w_in.shape[2] == 4 * hgw + 3 * mbw + 2 * d and hgw == mbw and d % hgw == 0
    alpha = (2 * depth) ** 0.25

    c_pad = jnp.pad(c, ((0, 8 - batch % 8 if batch % 8 else 0), (0, 0)))
    mod_rows = _adaln(c_pad, ada_w, ada_b, 0)

    def ffn_w(l, j):
        return [(ffn_w_gate, (l, j)), (ffn_w_up, (l, j)), (ffn_w_down, (l, j))]

    def branch_w(l):
        return [(w_branch_a, (l,)), (w_branch_b, (l,)), (w_out, (l,))]

    ffn0 = [w[0, 0].astype(BF16) for w in (ffn_w_gate, ffn_w_up, ffn_w_down)]
    win = w_in[0].astype(BF16)
    wa, wb, wo = [w[0].astype(BF16) for w in (w_branch_a, w_branch_b, w_out)]

    x2 = x.reshape(batch * seq, d)
    for l in range(depth):
        more = l + 1 < depth
        mod_l = mod_rows[:batch].reshape(batch, 3 * N_SUB, d)
        lng = [ln_g[l, j].reshape(1, d) for j in range(N_SUB)]
        lnb = [ln_b[l, j].reshape(1, d) for j in range(N_SUB)]
        x2 = _ffn(x2, mod_l, *ffn0, lng[0], lnb[0], layer=l, which=0, sub=0, seq=seq, alpha=alpha)
        (qf, lf, kk, vv, sg), cast_h = _proj_hg(x2, mod_l, hg_lb_logits, win,
                                               ffn_w(l, 1) + (branch_w(l + 1) if more else []),
                                               layer=l, seq=seq, hgw=hgw)
        (mq, mk, mv), cast_m = _proj_mb(x2, mod_l, win, ffn_w(l + 1, 0) if more else [],
                                        layer=l, seq=seq, mbw=mbw, col0=4 * hgw // mbw)
        oa, cast_g = _hgrn(qf, lf, kk, vv, sg, hg_norm_g, [(w_in, (l + 1,))] if more else [],
                           layer=l, batch=batch, seq=seq)
        ob, mod_rows = _moba(mq, mk, mv, (c_pad, ada_w, ada_b) if more else None, layer=l, batch=batch, seq=seq)
        x2 = _merge(x2, mod_l, oa, ob, win, wa, wb, wo, lng[1], lnb[1], layer=l, seq=seq, alpha=alpha)
        x2 = _ffn(x2, mod_l, *cast_h[:3], lng[2], lnb[2], layer=l, which=1, sub=2, seq=seq, alpha=alpha)
        if more:
            ffn0, (wa, wb, wo), (win,) = cast_m, cast_h[3:], cast_g
    return x2.reshape(batch, seq, d)
```

```python
import functools
import math

import jax
import jax.numpy as jnp
from jax import lax
from jax.experimental import pallas as pl
from jax.experimental.pallas import tpu as pltpu

F32 = jnp.float32
BF16 = jnp.bfloat16

HEAD = 128
MB_BLOCK = 256
MB_TOPK = 3
MB_TILE_BLOCKS = 2
MB_CHAIN_BLOCKS = 2
VT_PAD = 16
HG_CHUNK = 128
HG_BLOCK = 1024
HG_UNROLL = 8
HG_HEADS_PER_STEP = 4
F_MIN = 1e-30
MACARON_W = 0.5
N_SUB = 3
LN_EPS = 1e-5
RMS_EPS = 1e-6
NEG = -1e30
LANES = 128
VMEM_LIMIT = 56 * 1024 * 1024
FFN_VMEM_LIMIT = 60 * 1024 * 1024
FFN_ROWS = 1024
FFN_DOWN_COLS = 512


def _cparams(sem):
    return pltpu.CompilerParams(dimension_semantics=sem, vmem_limit_bytes=VMEM_LIMIT)


def _dot(a, b):
    return jnp.dot(a, b, preferred_element_type=F32)


def _dot_nt(a, b):
    return lax.dot_general(a, b, (((1,), (1,)), ((), ())), preferred_element_type=F32)


def _dot_tn(a, b):
    return lax.dot_general(a, b, (((0,), (0,)), ((), ())), preferred_element_type=F32)


def _silu(x):
    return x * jax.nn.sigmoid(x)


def _layer_norm(z, g, b):
    mu = jnp.mean(z, axis=-1, keepdims=True)
    zc = z - mu
    var = jnp.mean(zc * zc, axis=-1, keepdims=True)
    return zc * lax.rsqrt(var + LN_EPS) * g + b


def _mod_rows(mod_ref, sub):
    return (mod_ref[0, 3 * sub:3 * sub + 1, :], mod_ref[0, 3 * sub + 1:3 * sub + 2, :],
            mod_ref[0, 3 * sub + 2:3 * sub + 3, :])


def _adaln_kernel(c_ref, w_ref, b_ref, o_ref):
    cond = _silu(c_ref[...])
    o_ref[...] = _dot(cond, w_ref[...]) + b_ref[...]


def _adaln_specs(c_pad, ada_w, layer, step, max_steps=None):
    _, d, n = ada_w.shape
    tn = min(1024, d)
    while max_steps is not None and (n % tn or n // tn > max_steps):
        tn += LANES
    last = n // tn - 1
    col = lambda *ids: jnp.minimum(step(*ids), last)
    in_specs = [pl.BlockSpec(c_pad.shape, lambda *ids: (0, 0)),
                pl.BlockSpec((None, d, tn), lambda *ids: (layer, 0, col(*ids))),
                pl.BlockSpec((None, 1, tn), lambda *ids: (layer, 0, col(*ids)))]
    out_spec = pl.BlockSpec((c_pad.shape[0], tn), lambda *ids: (0, col(*ids)))
    return in_specs, out_spec, jax.ShapeDtypeStruct((c_pad.shape[0], n), F32), n // tn


def _adaln(c_pad, ada_w, ada_b, layer):
    depth, _, n = ada_w.shape
    in_specs, out_spec, out_shape, nsteps = _adaln_specs(c_pad, ada_w, layer, lambda j: j)
    return pl.pallas_call(
        _adaln_kernel,
        out_shape=out_shape,
        grid=(nsteps,),
        in_specs=in_specs,
        out_specs=out_spec,
        compiler_params=_cparams(("arbitrary",)),
        name="adaln",
    )(c_pad, ada_w, ada_b.reshape(depth, 1, n))


def _side_rows(r, nsteps):
    rb = 16 * -(-r // (16 * nsteps))
    while rb < r and r % rb:
        rb += 16
    return min(rb, r)


def _side_specs(side, nsteps, step):
    arrays, in_specs, out_specs, out_shapes = [], [], [], []
    for arr, lead in side:
        r, c = arr.shape[-2:]
        rb = _side_rows(r, nsteps)
        last = r // rb - 1
        arrays.append(arr)
        in_specs.append(pl.BlockSpec(
            (None,) * len(lead) + (rb, c),
            lambda *ids, lead=lead, last=last: lead + (jnp.minimum(step(*ids), last), 0)))
        out_specs.append(pl.BlockSpec((rb, c), lambda *ids, last=last: (jnp.minimum(step(*ids), last), 0)))
        out_shapes.append(jax.ShapeDtypeStruct((r, c), BF16))
    return arrays, in_specs, out_specs, out_shapes


def _cast_side(side_in, side_out):
    for src, dst in zip(side_in, side_out):
        dst[...] = src[...].astype(BF16)


def _ffn_kernel(x_hbm, mod_ref, wg_ref, wu_ref, wd_ref, lng_ref, lnb_ref, o_ref, h_scr, acc_scr, x_buf, x_sem,
                *, sub, alpha, nf, nm, tm):
    m = pl.program_id(0)
    f = pl.program_id(1)
    shift, scale, gate = _mod_rows(mod_ref, sub)

    def x_copy(tile):
        return pltpu.make_async_copy(x_hbm.at[pl.ds(tile * tm, tm), :], x_buf, x_sem)

    @pl.when((m == 0) & (f == 0))
    def _():
        x_copy(0).start()

    @pl.when(f == 0)
    def _():
        x_copy(m).wait()
        x = x_buf[...]
        h_scr[...] = (x * (1.0 + scale) + shift).astype(BF16)
        acc_scr[...] = alpha * x

    prefetch_step = min(1, nf - 1)

    @pl.when((f == prefetch_step) & (m + 1 < nm))
    def _():
        x_copy(m + 1).start(priority=1)

    h = h_scr[...]
    g = _dot(h, wg_ref[...])
    u = _dot(h, wu_ref[...])
    a = (_silu(g) * u).astype(BF16)
    cvec = MACARON_W * (1.0 + gate)
    d = acc_scr.shape[1]
    cw = FFN_DOWN_COLS if d % FFN_DOWN_COLS == 0 else d
    for n in range(d // cw):
        cols = slice(n * cw, (n + 1) * cw)
        acc_scr[:, cols] += _dot(a, wd_ref[:, cols]) * cvec[:, cols]

    @pl.when(f == nf - 1)
    def _():
        o_ref[...] = _layer_norm(acc_scr[...], lng_ref[...], lnb_ref[...])


def _ffn(x2, mod_l, wg, wu, wd, lng, lnb, *, layer, which, sub, seq, alpha):
    t, d = x2.shape
    dff = wg.shape[-1]
    tm = min(FFN_ROWS, seq)
    tf = 512 if dff % 512 == 0 else dff
    nf = dff // tf
    nm = t // tm
    spt = seq // tm
    return pl.pallas_call(
        functools.partial(_ffn_kernel, sub=sub, alpha=alpha, nf=nf, nm=nm, tm=tm),
        out_shape=jax.ShapeDtypeStruct((t, d), F32),
        grid=(nm, nf),
        in_specs=[pl.BlockSpec(memory_space=pl.ANY),
                  pl.BlockSpec((1, 3 * N_SUB, d), lambda m, f: (m // spt, 0, 0)),
                  pl.BlockSpec((d, tf), lambda m, f: (0, f)),
                  pl.BlockSpec((d, tf), lambda m, f: (0, f)),
                  pl.BlockSpec((tf, d), lambda m, f: (f, 0)),
                  pl.BlockSpec((1, d), lambda m, f: (0, 0)),
                  pl.BlockSpec((1, d), lambda m, f: (0, 0))],
        out_specs=pl.BlockSpec((tm, d), lambda m, f: (m, 0)),
        scratch_shapes=[pltpu.VMEM((tm, d), BF16), pltpu.VMEM((tm, d), F32), pltpu.VMEM((tm, d), F32),
                        pltpu.SemaphoreType.DMA(())],
        compiler_params=pltpu.CompilerParams(dimension_semantics=("arbitrary", "arbitrary"),
                                             vmem_limit_bytes=FFN_VMEM_LIMIT),
        name=f"ffn_l{layer}_{which}",
    )(x2, mod_l, wg, wu, wd, lng, lnb)


def _proj_hg_kernel(*refs, layer, nside):
    x_ref, mod_ref, lbl_ref, wq_ref, wf_ref, wi_ref, wg_ref = refs[:7]
    q_ref, lf_ref, k_ref, v_ref, sg_ref = refs[7 + nside:12 + nside]
    _cast_side(refs[7:7 + nside], refs[12 + nside:])
    shift, scale, _ = _mod_rows(mod_ref, 1)
    h = (x_ref[...] * (1.0 + scale) + shift).astype(BF16)
    logits = lbl_ref[...]
    e = jnp.exp(logits - jnp.max(logits, axis=0, keepdims=True))
    p = e / jnp.sum(e, axis=0, keepdims=True)
    lb = jnp.zeros_like(p[0:1, :])
    for i in range(1, layer + 1):
        lb = lb + p[i:i + 1, :]
    q_ref[...] = _silu(_dot(h, wq_ref[...])).astype(BF16)
    z = _dot(h, wf_ref[...])
    ez = jnp.exp(-jnp.abs(z))
    r = 1.0 / (1.0 + ez)
    er = ez * r
    pos = z >= 0.0
    fgate = lb + (1.0 - lb) * jnp.where(pos, r, er)
    lf_ref[...] = jnp.log2(jnp.maximum(fgate, F_MIN))
    k_ref[...] = ((1.0 - lb) * jnp.where(pos, er, r)).astype(BF16)
    v_ref[...] = _dot(h, wi_ref[...]).astype(BF16)
    sg_ref[...] = _silu(_dot(h, wg_ref[...])).astype(BF16)


def _proj_hg(x2, mod_l, lb_logits, w_in, side, *, layer, seq, hgw):
    t, d = x2.shape
    tm = min(512, seq)
    spt = seq // tm
    depth = lb_logits.shape[0]
    side_arrays, side_in, side_out, side_shapes = _side_specs(side, t // tm, lambda m: m)

    def wspec(col):
        return pl.BlockSpec((d, hgw), lambda m, col=col: (0, col), pipeline_mode=pl.Buffered(1))

    out = jax.ShapeDtypeStruct((t, hgw), BF16)
    ospec = pl.BlockSpec((tm, hgw), lambda m: (m, 0))
    outs = pl.pallas_call(
        functools.partial(_proj_hg_kernel, layer=layer, nside=len(side)),
        out_shape=[out, jax.ShapeDtypeStruct((t, hgw), F32), out, out, out] + side_shapes,
        grid=(t // tm,),
        in_specs=[pl.BlockSpec((tm, d), lambda m: (m, 0)),
                  pl.BlockSpec((1, 3 * N_SUB, d), lambda m: (m // spt, 0, 0)),
                  pl.BlockSpec((depth, hgw), lambda m: (0, 0)),
                  wspec(0), wspec(1), wspec(2), wspec(3)] + side_in,
        out_specs=[ospec, ospec, ospec, ospec, ospec] + side_out,
        compiler_params=_cparams(("arbitrary",)),
        name=f"proj_hg_l{layer}",
    )(x2, mod_l, lb_logits, w_in, w_in, w_in, w_in, *side_arrays)
    return outs[:5], outs[5:]


def _proj_mb_kernel(*refs, nside):
    x_ref, mod_ref, wq_ref, wk_ref, wv_ref = refs[:5]
    q_ref, k_ref, v_ref = refs[5 + nside:8 + nside]
    _cast_side(refs[5:5 + nside], refs[8 + nside:])
    shift, scale, _ = _mod_rows(mod_ref, 1)
    h = (x_ref[...] * (1.0 + scale) + shift).astype(BF16)
    q_ref[...] = _dot(h, wq_ref[...]).astype(BF16)
    k_ref[...] = _dot(h, wk_ref[...]).astype(BF16)
    v_ref[...] = _dot(h, wv_ref[...]).astype(BF16)


def _proj_mb(x2, mod_l, w_in, side, *, layer, seq, mbw, col0):
    t, d = x2.shape
    tm = min(512, seq)
    spt = seq // tm
    side_arrays, side_in, side_out, side_shapes = _side_specs(side, t // tm, lambda m: m)

    def wspec(col):
        return pl.BlockSpec((d, mbw), lambda m, col=col: (0, col), pipeline_mode=pl.Buffered(1))

    out = jax.ShapeDtypeStruct((t, mbw), BF16)
    ospec = pl.BlockSpec((tm, mbw), lambda m: (m, 0))
    outs = pl.pallas_call(
        functools.partial(_proj_mb_kernel, nside=len(side)),
        out_shape=[out, out, out] + side_shapes,
        grid=(t // tm,),
        in_specs=[pl.BlockSpec((tm, d), lambda m: (m, 0)),
                  pl.BlockSpec((1, 3 * N_SUB, d), lambda m: (m // spt, 0, 0)),
                  wspec(col0), wspec(col0 + 1), wspec(col0 + 2)] + side_in,
        out_specs=[ospec, ospec, ospec] + side_out,
        compiler_params=_cparams(("arbitrary",)),
        name=f"proj_mb_l{layer}",
    )(x2, mod_l, w_in, w_in, w_in, *side_arrays)
    return outs[:3], outs[3:]


def _bcast_left_end(w, c, row):
    n = w.shape[0]
    if c >= 8:
        parts = []
        for g in range(n // (2 * c)):
            r = g * 2 * c + c - 1
            parts.append(jnp.broadcast_to(w[r:r + 1, :], (2 * c, w.shape[1])))
        return parts[0] if len(parts) == 1 else jnp.concatenate(parts, axis=0)
    up = lambda x, s: pltpu.roll(x, n - s, 0)
    down = lambda x, s: pltpu.roll(x, s, 0)
    if c == 1:
        return jnp.where((row & 1) == 1, down(w, 1), w)
    if c == 2:
        z = jnp.where((row & 3) == 0, up(w, 1), w)
        return jnp.where((row & 3) >= 2, down(z, 2), z)
    z = jnp.where((row & 7) == 2, up(w, 1), w)
    z = jnp.where((row & 7) < 2, up(z, 2), z)
    return jnp.where((row & 7) >= 4, down(z, 4), z)


def _hgrn_kernel(*refs, chunk, nchunk, nh, nside):
    q_ref, lf_ref, k_ref, v_ref, sg_ref, ng_ref = refs[:6]
    o_ref = refs[6 + nside]
    st_scr = refs[-1]

    @pl.when(pl.program_id(2) == 0)
    def _():
        st_scr[...] = jnp.zeros_like(st_scr)

    _cast_side(refs[6:6 + nside], refs[7 + nside:-1])

    row = lax.broadcasted_iota(jnp.int32, (chunk, HEAD), 0)
    rr = lax.broadcasted_iota(jnp.int32, (chunk, chunk), 0)
    cc = lax.broadcasted_iota(jnp.int32, (chunk, chunk), 1)
    nlev = int(math.log2(chunk))
    pair = [((rr >> (lev + 1)) == (cc >> (lev + 1))) & ((rr & (1 << lev)) != 0) & ((cc & (1 << lev)) == 0)
            for lev in range(nlev)]

    def one_head(rows, hh):
        lanes = slice(hh * HEAD, (hh + 1) * HEAD)
        w = lf_ref[rows, lanes]
        qb = q_ref[rows, lanes]
        kb = k_ref[rows, lanes]
        q = qb.astype(F32)
        k = kb.astype(F32)
        v = v_ref[rows, lanes]
        attn = jnp.where(rr == cc, _dot_nt(qb, kb), 0.0)
        for lev in range(nlev):
            c = 1 << lev
            right = (row & c) != 0
            wl = _bcast_left_end(w, c, row)
            e = jnp.exp2(jnp.where(right, w, wl - w))
            p = _dot_nt((q * e).astype(BF16), (k * e).astype(BF16))
            attn = attn + jnp.where(pair[lev], p, 0.0)
            w = jnp.where(right, w + wl, w)
        bl = w[chunk - 1:chunk, :]
        qe = (q * jnp.exp2(w)).astype(BF16)
        ke = (k * jnp.exp2(bl - w)).astype(BF16)
        st = st_scr[hh]
        o = _dot(attn.astype(BF16), v) + _dot_nt(qe, st.astype(BF16))
        st_scr[hh] = st * jnp.exp2(bl) + _dot_tn(v, ke)
        o = o * lax.rsqrt(jnp.mean(o * o, axis=-1, keepdims=True) + RMS_EPS)
        o = o * ng_ref[:, lanes] * sg_ref[rows, lanes].astype(F32)
        o_ref[rows, lanes] = o.astype(BF16)

    def body(ci, carry):
        rows = pl.ds(pl.multiple_of(ci * chunk, chunk), chunk)
        for hh in range(nh):
            one_head(rows, hh)
        return carry

    lax.fori_loop(0, nchunk, body, 0, unroll=min(HG_UNROLL, nchunk))


def _hgrn(qf, lf, kk, vv, sg, norm_g, side, *, layer, batch, seq):
    t, hgw = qf.shape
    heads = hgw // HEAD
    nh = math.gcd(heads, HG_HEADS_PER_STEP)
    lb = min(HG_BLOCK, seq)
    chunk = min(HG_CHUNK, lb)
    nsb = seq // lb
    nhp = heads // nh
    side_arrays, side_in, side_out, side_shapes = _side_specs(
        side, batch * nhp * nsb, lambda b, h, s: (b * nhp + h) * nsb + s)
    spec = pl.BlockSpec((lb, nh * HEAD), lambda b, h, s: (b * nsb + s, h))
    outs = pl.pallas_call(
        functools.partial(_hgrn_kernel, chunk=chunk, nchunk=lb // chunk, nh=nh, nside=len(side)),
        out_shape=[jax.ShapeDtypeStruct((t, hgw), BF16)] + side_shapes,
        grid=(batch, nhp, nsb),
        in_specs=[spec, spec, spec, spec, spec,
                  pl.BlockSpec((None, 1, nh * HEAD), lambda b, h, s: (layer, 0, h))] + side_in,
        out_specs=[spec] + side_out,
        scratch_shapes=[pltpu.VMEM((nh, HEAD, HEAD), F32)],
        compiler_params=_cparams(("arbitrary", "arbitrary", "arbitrary")),
        name=f"hgrn_l{layer}",
    )(qf, lf, kk, vv, sg, norm_g.reshape(norm_g.shape[0], 1, hgw), *side_arrays)
    return outs[0], outs[1:]


def _moba_kernel(*refs, nblk, blk, nqb, with_mod):
    q_ref, k_ref, v_ref = refs[:3]
    if with_mod:
        o_ref, mod_ref = refs[6:8]
        _adaln_kernel(*refs[3:6], mod_ref)
    else:
        o_ref = refs[3]
    kmh_scr, kml_scr, vt_scr = refs[-3:]
    tq = nqb * blk
    ntile = nblk // nqb
    nrow = kmh_scr.shape[0]
    seq = nblk * blk

    kmh_scr[...] = jnp.zeros_like(kmh_scr)
    kml_scr[...] = jnp.zeros_like(kml_scr)
    for j in range(nblk):
        km = jnp.sum(k_ref[j * blk:(j + 1) * blk, :].astype(F32), axis=0, keepdims=True) * (1.0 / blk)
        hi = km.astype(BF16)
        kmh_scr[j:j + 1, :] = hi
        kml_scr[j:j + 1, :] = (km - hi.astype(F32)).astype(BF16)
    ones_rows = (lax.broadcasted_iota(jnp.int32, (VT_PAD, tq), 0) == 0).astype(BF16)
    for g in range(ntile):
        vt_scr[g, :HEAD, :] = v_ref[g * tq:(g + 1) * tq, :].astype(F32).T.astype(BF16)
        vt_scr[g, HEAD:, :] = ones_rows

    scale = HEAD ** -0.5
    q_all = q_ref[...]
    gate = _dot_nt(kmh_scr[...], q_all) + _dot_nt(kml_scr[...], q_all)
    kb = lax.broadcasted_iota(jnp.int32, (nrow, seq), 0).astype(F32)
    qb = (lax.broadcasted_iota(jnp.int32, (1, seq), 1) // blk).astype(F32)
    g_ = jnp.where(kb < qb, gate, NEG)
    picks = []
    for _ in range(MB_TOPK):
        m = jnp.max(g_, axis=0, keepdims=True)
        idx = jnp.min(jnp.where(g_ == m, kb, float(nrow)), axis=0, keepdims=True)
        picks.append(jnp.where(idx < qb, idx, -1.0))
        g_ = jnp.where(kb == idx, -3e38, g_)

    c = scale * math.log2(math.e)
    tri = (lax.broadcasted_iota(jnp.int32, (blk, blk), 0) <= lax.broadcasted_iota(jnp.int32, (blk, blk), 1))
    ncb = math.gcd(nqb, MB_CHAIN_BLOCKS)
    nch = nqb // ncb
    cw = ncb * blk

    def scores(it, ch, g):
        own = g == it
        e0 = ch * ncb
        nkb = e0 + ncb if own else nqb
        q0 = (it * nqb + e0) * blk
        s = _dot_nt(k_ref[g * tq:g * tq + nkb * blk, :], q_ref[q0:q0 + cw, :])
        parts = []
        for d in range(nkb):
            sd = s[d * blk:(d + 1) * blk, :]
            j_f = float(g * nqb + d)
            cols = []
            for e in range(e0, e0 + ncb):
                sde = sd[:, (e - e0) * blk:(e - e0 + 1) * blk]
                lanes = slice((it * nqb + e) * blk, (it * nqb + e + 1) * blk)
                if own and d == e:
                    cols.append(jnp.where(tri, sde, NEG))
                elif own and d > e:
                    cols.append(jnp.full_like(sde, NEG))
                else:
                    sel = (picks[0][:, lanes] == j_f) | (picks[1][:, lanes] == j_f) | (picks[2][:, lanes] == j_f)
                    cols.append(sde + jnp.where(sel, 0.0, NEG))
            parts.append(cols[0] if ncb == 1 else jnp.concatenate(cols, axis=1))
        return parts[0] if nkb == 1 else jnp.concatenate(parts, axis=0)

    def update(g, s, carry):
        m_prev, acc = carry
        m_new = jnp.maximum(m_prev, jnp.max(s, axis=0, keepdims=True))
        a = jnp.exp2((m_prev - m_new) * c)
        p = jnp.exp2((s - m_new) * c)
        acc = a * acc + _dot(vt_scr[g, :, :s.shape[0]], p.astype(BF16))
        return m_new, acc

    steps = [(it, g) for it in range(ntile) for g in range(it + 1)]
    init = (jnp.full((1, cw), NEG, F32), jnp.zeros((HEAD + VT_PAD, cw), F32))
    cur = [scores(steps[0][0], ch, steps[0][1]) for ch in range(nch)]
    carry = [init] * nch
    for i, (it, g) in enumerate(steps):
        nxt = [scores(steps[i + 1][0], ch, steps[i + 1][1]) for ch in range(nch)] if i + 1 < len(steps) else None
        carry = [update(g, cur[ch], carry[ch]) for ch in range(nch)]
        if g == it:
            outs = [acc[:HEAD, :] / acc[HEAD:HEAD + 1, :] for _, acc in carry]
            out = outs[0] if nch == 1 else jnp.concatenate(outs, axis=1)
            o_ref[it * tq:(it + 1) * tq, :] = out.T.astype(BF16)
            carry = [init] * nch
        cur = nxt


def _moba(mq, mk, mv, ada, *, layer, batch, seq):
    t, mbw = mq.shape
    heads = mbw // HEAD
    blk = MB_BLOCK
    nblk = seq // blk
    nqb = math.gcd(nblk, MB_TILE_BLOCKS)
    assert seq % blk == 0
    nrow = -(-nblk // 16) * 16
    spec = pl.BlockSpec((seq, HEAD), lambda b, h: (b, h))
    in_specs, out_specs, out_shape, args = [spec, spec, spec], [spec], [jax.ShapeDtypeStruct((t, mbw), BF16)], []
    if ada is not None:
        c_pad, ada_w, ada_b = ada
        a_in, a_out, a_shape, _ = _adaln_specs(c_pad, ada_w, layer + 1, lambda b, h: b * heads + h, batch * heads)
        in_specs, out_specs, out_shape = in_specs + a_in, out_specs + [a_out], out_shape + [a_shape]
        args = [c_pad, ada_w, ada_b.reshape(ada_w.shape[0], 1, ada_w.shape[2])]
    outs = pl.pallas_call(
        functools.partial(_moba_kernel, nblk=nblk, blk=blk, nqb=nqb, with_mod=ada is not None),
        out_shape=out_shape,
        grid=(batch, heads),
        in_specs=in_specs,
        out_specs=out_specs,
        scratch_shapes=[pltpu.VMEM((nrow, HEAD), BF16), pltpu.VMEM((nrow, HEAD), BF16),
                        pltpu.VMEM((nblk // nqb, HEAD + VT_PAD, nqb * blk), BF16)],
        compiler_params=_cparams(("arbitrary", "arbitrary")),
        name=f"moba_l{layer}",
    )(mq, mk, mv, *args)
    return outs[0], (outs[1] if ada is not None else None)


def _merge_kernel(*refs, alpha, ng):
    x_ref, mod_ref, oa_ref, ob_ref = refs[:4]
    wga_refs, wgb_refs = refs[4:4 + ng], refs[4 + ng:4 + 2 * ng]
    wa_ref, wb_ref, wo_ref, lng_ref, lnb_ref, o_ref = refs[4 + 2 * ng:]
    shift, scale, gate = _mod_rows(mod_ref, 1)
    x = x_ref[...]
    h = (x * (1.0 + scale) + shift).astype(BF16)

    def gates(w_refs):
        parts = [jax.nn.sigmoid(_dot(h, w[...])) for w in w_refs]
        return parts[0] if ng == 1 else jnp.concatenate(parts, axis=1)

    merged = gates(wga_refs) * _dot(oa_ref[...], wa_ref[...])
    merged = merged + gates(wgb_refs) * _dot(ob_ref[...], wb_ref[...])
    y = _dot(merged.astype(BF16), wo_ref[...])
    z = alpha * x + (1.0 + gate) * y
    o_ref[...] = _layer_norm(z, lng_ref[...], lnb_ref[...])


def _merge(x2, mod_l, oa, ob, w_in, wa, wb, wo, lng, lnb, *, layer, seq, alpha):
    t, d = x2.shape
    hgw, mbw = oa.shape[1], ob.shape[1]
    tm = min(256, seq)
    spt = seq // tm
    one = pl.Buffered(1)
    gw = math.gcd(d, w_in.shape[1] - 2 * d)
    ng = d // gw
    g0 = (w_in.shape[1] - 2 * d) // gw
    gspecs = [pl.BlockSpec((d, gw), lambda m, col=g0 + i: (0, col), pipeline_mode=one) for i in range(2 * ng)]
    return pl.pallas_call(
        functools.partial(_merge_kernel, alpha=alpha, ng=ng),
        out_shape=jax.ShapeDtypeStruct((t, d), F32),
        grid=(t // tm,),
        in_specs=[pl.BlockSpec((tm, d), lambda m: (m, 0)),
                  pl.BlockSpec((1, 3 * N_SUB, d), lambda m: (m // spt, 0, 0)),
                  pl.BlockSpec((tm, hgw), lambda m: (m, 0)),
                  pl.BlockSpec((tm, mbw), lambda m: (m, 0))] + gspecs + [
                  pl.BlockSpec((hgw, d), lambda m: (0, 0), pipeline_mode=one),
                  pl.BlockSpec((mbw, d), lambda m: (0, 0), pipeline_mode=one),
                  pl.BlockSpec((d, d), lambda m: (0, 0), pipeline_mode=one),
                  pl.BlockSpec((1, d), lambda m: (0, 0)),
                  pl.BlockSpec((1, d), lambda m: (0, 0))],
        out_specs=pl.BlockSpec((tm, d), lambda m: (m, 0)),
        compiler_params=_cparams(("parallel",)),
        name=f"merge_l{layer}",
    )(x2, mod_l, oa, ob, *([w_in] * (2 * ng)), wa, wb, wo, lng, lnb)


def kernel(x, c, ada_w, ada_b, ln_g, ln_b, ffn_w_gate, ffn_w_up, ffn_w_down, w_in, hg_lb_logits,
           hg_norm_g, w_branch_a, w_branch_b, w_out):
    batch, seq, d = x.shape
    depth = ada_w.shape[0]
    hgw = hg_norm_g.shape[1]
    mbw = w_branch_b.shape[1]
    assert w_in.shape[2] == 4 * hgw + 3 * mbw + 2 * d and hgw == mbw and d % hgw == 0
    alpha = (2 * depth) ** 0.25

    c_pad = jnp.pad(c, ((0, 8 - batch % 8 if batch % 8 else 0), (0, 0)))
    mod_rows = _adaln(c_pad, ada_w, ada_b, 0)

    def ffn_w(l, j):
        return [(ffn_w_gate, (l, j)), (ffn_w_up, (l, j)), (ffn_w_down, (l, j))]

    def branch_w(l):
        return [(w_branch_a, (l,)), (w_branch_b, (l,)), (w_out, (l,))]

    ffn0 = [w[0, 0].astype(BF16) for w in (ffn_w_gate, ffn_w_up, ffn_w_down)]
    win = w_in[0].astype(BF16)
    wa, wb, wo = [w[0].astype(BF16) for w in (w_branch_a, w_branch_b, w_out)]

    x2 = x.reshape(batch * seq, d)
    for l in range(depth):
        more = l + 1 < depth
        mod_l = mod_rows[:batch].reshape(batch, 3 * N_SUB, d)
        lng = [ln_g[l, j].reshape(1, d) for j in range(N_SUB)]
        lnb = [ln_b[l, j].reshape(1, d) for j in range(N_SUB)]
        x2 = _ffn(x2, mod_l, *ffn0, lng[0], lnb[0], layer=l, which=0, sub=0, seq=seq, alpha=alpha)
        (qf, lf, kk, vv, sg), cast_h = _proj_hg(x2, mod_l, hg_lb_logits, win,
                                               ffn_w(l, 1) + (branch_w(l + 1) if more else []),
                                               layer=l, seq=seq, hgw=hgw)
        (mq, mk, mv), cast_m = _proj_mb(x2, mod_l, win, ffn_w(l + 1, 0) if more else [],
                                        layer=l, seq=seq, mbw=mbw, col0=4 * hgw // mbw)
        oa, cast_g = _hgrn(qf, lf, kk, vv, sg, hg_norm_g, [(w_in, (l + 1,))] if more else [],
                           layer=l, batch=batch, seq=seq)
        ob, mod_rows = _moba(mq, mk, mv, (c_pad, ada_w, ada_b) if more else None, layer=l, batch=batch, seq=seq)
        x2 = _merge(x2, mod_l, oa, ob, win, wa, wb, wo, lng[1], lnb[1], layer=l, seq=seq, alpha=alpha)
        x2 = _ffn(x2, mod_l, *cast_h[:3], lng[2], lnb[2], layer=l, which=1, sub=2, seq=seq, alpha=alpha)
        if more:
            ffn0, (wa, wb, wo), (win,) = cast_m, cast_h[3:], cast_g
    return x2.reshape(batch, seq, d)
```
